```python
import math
import jax, jax.numpy as jnp
from jax import lax
import numpy as np

D_MODEL = 1024
BATCH = 8
SEQ = 2048
DEPTH = 2
DEC_BATCH = 128
DEC_SEQ = 1
PAST_LEN = 8192
PAGE_SIZE = 128

N_META = 16
Q_BLOCK = 128
ROPE_THETA = 10000.0
EPS = 1e-6
NEG = -1e30

SB_HEADS = 4
SB_KV_HEADS = 2
SB_HD = 64
DF_HEADS = 4
DF_KV_HEADS = 2
DF_HD = 64
DF_VD = 2 * DF_HD
FX_HEADS = 4
FX_KV_HEADS = 2
FX_HD = 64
FX_FORGET_BIAS = 5.0
MLA_HEADS = 4
MLA_Q_LORA = 256
MLA_KV_LORA = 128
MLA_NOPE = 64
MLA_ROPE = 32
MLA_VD = 64
MLA_QK = MLA_NOPE + MLA_ROPE

SB_GROUP = SB_HEADS // SB_KV_HEADS
DF_GROUP = DF_HEADS // DF_KV_HEADS
FX_GROUP = FX_HEADS // FX_KV_HEADS
SB_OUT = SB_HEADS * SB_HD
DF_OUT = DF_HEADS * DF_VD
FX_OUT = FX_HEADS * FX_HD
MLA_OUT = MLA_HEADS * MLA_VD
MIX_WIDTH = SB_OUT + DF_OUT + FX_OUT + MLA_OUT
N_BRANCH = 4
IN_SPLITS = (
    SB_HEADS * SB_HD, 2 * SB_KV_HEADS * SB_HD,
    2 * DF_HEADS * DF_HD, 2 * DF_KV_HEADS * DF_HD, DF_KV_HEADS * DF_VD,
    FX_HEADS * FX_HD, 2 * FX_KV_HEADS * FX_HD, FX_HEADS,
    MLA_Q_LORA, MLA_KV_LORA, MLA_ROPE,
)
IN_WIDTH = sum(IN_SPLITS)

N_GROUPS = 4
EXPERTS_PER_GROUP = 4
N_EXPERTS = N_GROUPS * EXPERTS_PER_GROUP
TOP_K = 2
EXP_HIDDEN = 256

kernel_name = 'hybrid_sb_diff_fox_mla_hmoe_step'


def rmsnorm(x, g):
    xf = x.astype(jnp.float32)
    y = xf * lax.rsqrt(jnp.mean(xf * xf, axis=-1, keepdims=True) + EPS)
    return (y * g.astype(jnp.float32)).astype(x.dtype)


def rope(x, pos):
    half = x.shape[-1] // 2
    inv_freq = ROPE_THETA ** (-jnp.arange(half, dtype=jnp.float32) / half)
    ang = pos.astype(jnp.float32)[:, None] * inv_freq[None, :]
    ang = ang.reshape((pos.shape[0],) + (1,) * (x.ndim - 3) + (half,))
    cos, sin = jnp.cos(ang), jnp.sin(ang)
    x1 = x[..., :half].astype(jnp.float32)
    x2 = x[..., half:].astype(jnp.float32)
    return jnp.concatenate([x1 * cos - x2 * sin, x2 * cos + x1 * sin], axis=-1).astype(x.dtype)


def sweep_queries(fn, qs, q_pos, lead):
    outs = []
    if lead:
        outs.append(fn(tuple(a[:, :lead] for a in qs), q_pos[:lead]))
    rest = q_pos.shape[0] - lead
    nb, rem = divmod(rest, Q_BLOCK)
    end = lead + nb * Q_BLOCK
    if nb:
        blocks = tuple(jnp.moveaxis(a[:, lead:end].reshape((a.shape[0], nb, Q_BLOCK) + a.shape[2:]), 1, 0) for a in qs)
        pos_b = q_pos[lead:end].reshape(nb, Q_BLOCK)
        o = lax.map(lambda t: fn(t[0], t[1]), (blocks, pos_b))
        outs.append(jnp.moveaxis(o, 0, 1).reshape(o.shape[1], nb * Q_BLOCK, o.shape[-1]))
    if rem:
        outs.append(fn(tuple(a[:, end:] for a in qs), q_pos[end:]))
    return outs[0] if len(outs) == 1 else jnp.concatenate(outs, axis=1)


def parallel_mixers(h, l, p, past, lead):
    B, Lq, _ = h.shape
    dt = h.dtype
    f32 = jnp.float32
    p0 = 0 if past is None else past[0].shape[1]
    q_pos = p0 + jnp.arange(Lq, dtype=jnp.int32)
    k_pos = jnp.arange(p0 + Lq, dtype=jnp.int32)
    parts = jnp.split(h @ p['w_in'][l], np.cumsum(IN_SPLITS)[:-1].tolist(), axis=-1)
    sb_q, sb_kv, df_q, df_k, df_v, fx_q, fx_kv, fx_f, mla_cq, mla_ckv, mla_kr = parts

    sb_q = sb_q.reshape(B, Lq, SB_KV_HEADS, SB_GROUP, SB_HD)
    sb_kv_new = sb_kv.reshape(B, Lq, 2, SB_KV_HEADS, SB_HD)
    df_q = rope(rmsnorm(df_q.reshape(B, Lq, 2, DF_KV_HEADS, DF_GROUP, DF_HD), p['df_q_norm'][l]), q_pos)
    df_k_new = rope(rmsnorm(df_k.reshape(B, Lq, 2, DF_KV_HEADS, DF_HD), p['df_k_norm'][l]), q_pos)
    df_v_new = df_v.reshape(B, Lq, DF_KV_HEADS, DF_VD)
    fx_q = rmsnorm(fx_q.reshape(B, Lq, FX_KV_HEADS, FX_GROUP, FX_HD), p['fx_q_norm'][l])
    fx_kv = fx_kv.reshape(B, Lq, 2, FX_KV_HEADS, FX_HD)
    fx_kv_new = jnp.stack([rmsnorm(fx_kv[:, :, 0], p['fx_k_norm'][l]), fx_kv[:, :, 1]], axis=2)
    fx_logf_new = jax.nn.log_sigmoid((fx_f + p['fx_forget_bias'][l]).astype(f32)).astype(dt)
    c_q = rmsnorm(mla_cq, p['mla_q_lora_norm'][l])
    q_full = jnp.einsum('blr,rhe->blhe', c_q, p['mla_w_uq'][l])
    mla_q = rmsnorm(jnp.concatenate([q_full[..., :MLA_NOPE], rope(q_full[..., MLA_NOPE:], q_pos)], axis=-1), p['mla_q_norm'][l])
    mla_lat_new = rmsnorm(mla_ckv, p['mla_kv_lora_norm'][l])
    mla_kr_new = rope(mla_kr[:, :, None, :], q_pos)[:, :, 0]

    rows = (sb_kv_new, df_k_new, df_v_new, fx_kv_new, fx_logf_new, mla_lat_new, mla_kr_new)
    if past is None:
        full = rows
    else:
        full = tuple(jnp.concatenate([a, r.astype(a.dtype)], axis=1) for a, r in zip(past, rows))
    sb_kv_all, df_k_all, df_v_all, fx_kv_all, fx_logf_all, mla_lat_all, mla_kr_all = full

    sb_k, sb_v = sb_kv_all[:, :, 0], sb_kv_all[:, :, 1]

    def sb_block(qs, pos):
        (q,) = qs
        z = jnp.einsum('bqhgd,bkhd->bhgqk', q, sb_k, preferred_element_type=f32) * SB_HD ** -0.5
        vis = k_pos[None, :] < pos[:, None]
        log_keep = jnp.where(vis, jax.nn.log_sigmoid(-z), 0.0)
        later = lax.cumsum(log_keep, axis=4, reverse=True) - log_keep
        w = jnp.where(vis, jnp.exp(jax.nn.log_sigmoid(z) + later), 0.0)
        o = jnp.einsum('bhgqk,bkhd->bqhgd', w.astype(sb_v.dtype), sb_v)
        return o.reshape(o.shape[0], o.shape[1], SB_OUT)

    sb_o = sweep_queries(sb_block, (sb_q,), q_pos, lead)

    lam_p = p['df_lambda'][l].astype(f32)
    lam_init = 0.8 - 0.6 * math.exp(-0.3 * l)
    lam = jnp.exp(jnp.sum(lam_p[0] * lam_p[1])) - jnp.exp(jnp.sum(lam_p[2] * lam_p[3])) + lam_init

    def df_block(qs, pos):
        (q,) = qs
        s = jnp.einsum('bqmhgd,bkmhd->bmhgqk', q, df_k_all, preferred_element_type=f32) * DF_HD ** -0.5
        vis = k_pos[None, :] <= pos[:, None]
        pr = jax.nn.softmax(jnp.where(vis, s, NEG), axis=-1)
        a = pr[:, 0] - lam * pr[:, 1]
        o = jnp.einsum('bhgqk,bkhe->bqhge', a.astype(df_v_all.dtype), df_v_all)
        o = rmsnorm(o, p['df_out_norm'][l]) * (1.0 - lam_init)
        return o.reshape(o.shape[0], o.shape[1], DF_OUT)

    df_o = sweep_queries(df_block, (df_q,), q_pos, lead)

    cum = lax.cumsum(fx_logf_all.astype(f32), axis=1)
    cum_k = jnp.moveaxis(cum.reshape(B, -1, FX_KV_HEADS, FX_GROUP), 1, -1)
    cum_q = cum[:, p0:]
    fx_k, fx_v = fx_kv_all[:, :, 0], fx_kv_all[:, :, 1]

    def fx_block(qs, pos):
        q, cq = qs
        s = jnp.einsum('bqhgd,bkhd->bhgqk', q, fx_k, preferred_element_type=f32) * FX_HD ** -0.5
        cq = jnp.moveaxis(cq.reshape(cq.shape[0], cq.shape[1], FX_KV_HEADS, FX_GROUP), 1, -1)
        bias = cq[..., None] - cum_k[..., None, :]
        vis = k_pos[None, :] <= pos[:, None]
        pr = jax.nn.softmax(jnp.where(vis, s + bias, NEG), axis=-1)
        o = jnp.einsum('bhgqk,bkhd->bqhgd', pr.astype(fx_v.dtype), fx_v)
        return o.reshape(o.shape[0], o.shape[1], FX_OUT)

    fx_o = sweep_queries(fx_block, (fx_q, cum_q), q_pos, lead)

    kv = jnp.einsum('bkr,rhe->bkhe', mla_lat_all, p['mla_w_ukv'][l])
    kr = jnp.broadcast_to(mla_kr_all[:, :, None, :].astype(kv.dtype), kv.shape[:3] + (MLA_ROPE,))
    mla_k = rmsnorm(jnp.concatenate([kv[..., :MLA_NOPE], kr], axis=-1), p['mla_k_norm'][l])
    mla_v = kv[..., MLA_NOPE:]

    def mla_block(qs, pos):
        (q,) = qs
        s = jnp.einsum('bqhe,bkhe->bhqk', q, mla_k, preferred_element_type=f32) * MLA_QK ** -0.5
        vis = k_pos[None, :] <= pos[:, None]
        pr = jax.nn.softmax(jnp.where(vis, s, NEG), axis=-1)
        o = jnp.einsum('bhqk,bkhd->bqhd', pr.astype(mla_v.dtype), mla_v)
        return o.reshape(o.shape[0], o.shape[1], MLA_OUT)

    mla_o = sweep_queries(mla_block, (mla_q,), q_pos, lead)

    gates = jax.nn.sigmoid((h @ p['w_gate'][l]).astype(f32)).astype(dt).reshape(B, Lq, N_BRANCH, D_MODEL)
    w_br = p['w_branch'][l]
    merged = jnp.zeros((B, Lq, D_MODEL), dt)
    off = 0
    for b, o in enumerate((sb_o, df_o, fx_o, mla_o)):
        width = o.shape[-1]
        merged = merged + gates[:, :, b] * (o @ w_br[off:off + width])
        off += width
    return merged @ p['w_out'][l], rows


def hier_moe(h, l, p):
    B, L, D = h.shape
    f32 = jnp.float32
    t = h.reshape(B * L, D)
    g_logit = (t @ p['moe_w_group'][l]).astype(f32) + p['moe_b_group'][l].astype(f32)
    g_idx = jnp.argmax(g_logit, axis=-1)
    g_w = jnp.max(jax.nn.softmax(g_logit, axis=-1), axis=-1)
    e_logit = ((t @ p['moe_w_router'][l]).astype(f32) + p['moe_b_router'][l].astype(f32)).reshape(-1, N_GROUPS, EXPERTS_PER_GROUP)
    e_sel = jnp.take_along_axis(e_logit, g_idx[:, None, None], axis=1)[:, 0]
    top_p, top_i = lax.top_k(jax.nn.softmax(e_sel, axis=-1), TOP_K)
    w = top_p / jnp.sum(top_p, axis=-1, keepdims=True) * g_w[:, None]
    eid = g_idx[:, None] * EXPERTS_PER_GROUP + top_i
    combine = jnp.sum(jax.nn.one_hot(eid, N_EXPERTS, dtype=f32) * w[..., None], axis=1)
    gu = jnp.einsum('nd,edf->nef', t, p['moe_w_gate_up'][l])
    act = jax.nn.silu(gu[..., :EXP_HIDDEN]) * gu[..., EXP_HIDDEN:] * combine[..., None].astype(h.dtype)
    out = jnp.einsum('nef,efd->nd', act, p['moe_w_down'][l])
    return out.reshape(B, L, D)


def trunk_layer(x, l, p, past, lead):
    mix, rows = parallel_mixers(rmsnorm(x, p['norm_mix'][l]), l, p, past, lead)
    x = x + mix
    x = x + hier_moe(rmsnorm(x, p['norm_ffn'][l]), l, p)
    return x, rows


def gather_past(cache, l, page_table):
    g = cache[l, page_table]
    return g.reshape((page_table.shape[0], page_table.shape[1] * cache.shape[2]) + cache.shape[3:])


def stack_layers(rows_list):
    return tuple(jnp.stack([r[i] for r in rows_list], axis=0) for i in range(len(rows_list[0])))


def setup_inputs(seed: int = 0) -> dict:
    key = jax.random.key(seed)
    keys = jax.random.split(key, 48)
    counter = [0]

    def next_key():
        k = keys[counter[0]]
        counter[0] += 1
        return k

    def nrm(shape, scale=1.0):
        return scale * jax.random.normal(next_key(), shape, jnp.float32)

    def gain(shape):
        return 1.0 + 0.01 * nrm(shape)

    n_pages = PAST_LEN // PAGE_SIZE
    n_used = DEC_BATCH * n_pages
    n_pool = n_used + max(1, n_used // 4)
    d = D_MODEL
    inp = {}
    inp['x_prompt'] = nrm((BATCH, SEQ, d))
    inp['x_sample'] = nrm((DEC_BATCH, DEC_SEQ, d))
    inp['cache_sb_kv'] = nrm((DEPTH, n_pool, PAGE_SIZE, 2, SB_KV_HEADS, SB_HD))
    inp['cache_df_k'] = nrm((DEPTH, n_pool, PAGE_SIZE, 2, DF_KV_HEADS, DF_HD))
    inp['cache_df_v'] = nrm((DEPTH, n_pool, PAGE_SIZE, DF_KV_HEADS, DF_VD))
    inp['cache_fx_kv'] = nrm((DEPTH, n_pool, PAGE_SIZE, 2, FX_KV_HEADS, FX_HD))
    inp['cache_fx_logf'] = jax.nn.log_sigmoid(FX_FORGET_BIAS + nrm((DEPTH, n_pool, PAGE_SIZE, FX_HEADS)))
    inp['cache_mla_latent'] = nrm((DEPTH, n_pool, PAGE_SIZE, MLA_KV_LORA))
    inp['cache_mla_krope'] = nrm((DEPTH, n_pool, PAGE_SIZE, MLA_ROPE))
    inp['page_table'] = jax.random.permutation(next_key(), n_pool)[:n_used].reshape(DEC_BATCH, n_pages).astype(jnp.int32)
    inp['meta_tokens'] = nrm((N_META, d))
    inp['norm_mix'] = gain((DEPTH, d))
    inp['norm_ffn'] = gain((DEPTH, d))
    inp['w_in'] = nrm((DEPTH, d, IN_WIDTH), d ** -0.5)
    inp['w_gate'] = nrm((DEPTH, d, N_BRANCH * d), d ** -0.5)
    inp['w_branch'] = nrm((DEPTH, MIX_WIDTH, d), SB_OUT ** -0.5)
    inp['w_out'] = nrm((DEPTH, d, d), d ** -0.5)
    inp['df_q_norm'] = gain((DEPTH, DF_HD))
    inp['df_k_norm'] = gain((DEPTH, DF_HD))
    inp['df_lambda'] = nrm((DEPTH, 4, DF_HD), 0.1)
    inp['df_out_norm'] = gain((DEPTH, DF_VD))
    inp['fx_q_norm'] = gain((DEPTH, FX_HD))
    inp['fx_k_norm'] = gain((DEPTH, FX_HD))
    inp['fx_forget_bias'] = FX_FORGET_BIAS + nrm((DEPTH, FX_HEADS), 0.1)
    inp['mla_q_lora_norm'] = gain((DEPTH, MLA_Q_LORA))
    inp['mla_kv_lora_norm'] = gain((DEPTH, MLA_KV_LORA))
    inp['mla_w_uq'] = nrm((DEPTH, MLA_Q_LORA, MLA_HEADS, MLA_QK), MLA_Q_LORA ** -0.5)
    inp['mla_w_ukv'] = nrm((DEPTH, MLA_KV_LORA, MLA_HEADS, MLA_NOPE + MLA_VD), MLA_KV_LORA ** -0.5)
    inp['mla_q_norm'] = gain((DEPTH, MLA_QK))
    inp['mla_k_norm'] = gain((DEPTH, MLA_QK))
    inp['moe_w_group'] = nrm((DEPTH, d, N_GROUPS), d ** -0.5)
    inp['moe_b_group'] = nrm((DEPTH, N_GROUPS), 0.01)
    inp['moe_w_router'] = nrm((DEPTH, d, N_EXPERTS), d ** -0.5)
    inp['moe_b_router'] = nrm((DEPTH, N_EXPERTS), 0.01)
    inp['moe_w_gate_up'] = nrm((DEPTH, N_EXPERTS, d, 2 * EXP_HIDDEN), d ** -0.5)
    inp['moe_w_down'] = nrm((DEPTH, N_EXPERTS, EXP_HIDDEN, d), EXP_HIDDEN ** -0.5)
    return inp


def reference(x_prompt, x_sample, cache_sb_kv, cache_df_k, cache_df_v, cache_fx_kv, cache_fx_logf,
              cache_mla_latent, cache_mla_krope, page_table, meta_tokens, norm_mix, norm_ffn, w_in,
              w_gate, w_branch, w_out, df_q_norm, df_k_norm, df_lambda, df_out_norm, fx_q_norm,
              fx_k_norm, fx_forget_bias, mla_q_lora_norm, mla_kv_lora_norm, mla_w_uq, mla_w_ukv,
              mla_q_norm, mla_k_norm, moe_w_group, moe_b_group, moe_w_router, moe_b_router,
              moe_w_gate_up, moe_w_down):
    p = {
        'norm_mix': norm_mix, 'norm_ffn': norm_ffn, 'w_in': w_in, 'w_gate': w_gate,
        'w_branch': w_branch, 'w_out': w_out, 'df_q_norm': df_q_norm, 'df_k_norm': df_k_norm,
        'df_lambda': df_lambda, 'df_out_norm': df_out_norm, 'fx_q_norm': fx_q_norm,
        'fx_k_norm': fx_k_norm, 'fx_forget_bias': fx_forget_bias,
        'mla_q_lora_norm': mla_q_lora_norm, 'mla_kv_lora_norm': mla_kv_lora_norm,
        'mla_w_uq': mla_w_uq, 'mla_w_ukv': mla_w_ukv, 'mla_q_norm': mla_q_norm,
        'mla_k_norm': mla_k_norm, 'moe_w_group': moe_w_group, 'moe_b_group': moe_b_group,
        'moe_w_router': moe_w_router, 'moe_b_router': moe_b_router,
        'moe_w_gate_up': moe_w_gate_up, 'moe_w_down': moe_w_down,
    }
    caches = (cache_sb_kv, cache_df_k, cache_df_v, cache_fx_kv, cache_fx_logf, cache_mla_latent, cache_mla_krope)

    meta = jnp.broadcast_to(meta_tokens[None].astype(x_prompt.dtype), (x_prompt.shape[0], N_META, D_MODEL))
    x = jnp.concatenate([meta, x_prompt], axis=1)
    rows_p = []
    for l in range(DEPTH):
        x, rows = trunk_layer(x, l, p, None, N_META)
        rows_p.append(rows)
    y_prompt = x[:, N_META:]

    xs = x_sample
    rows_s = []
    for l in range(DEPTH):
        past = tuple(gather_past(c, l, page_table) for c in caches)
        xs, rows = trunk_layer(xs, l, p, past, 0)
        rows_s.append(rows)
    y_sample = xs

    sb_kv_p, df_k_p, df_v_p, fx_kv_p, fx_logf_p, mla_lat_p, mla_kr_p = stack_layers(rows_p)
    sb_kv_s, df_k_s, df_v_s, fx_kv_s, fx_logf_s, mla_lat_s, mla_kr_s = stack_layers(rows_s)
    return (y_prompt, y_sample, sb_kv_p, sb_kv_s, df_k_p, df_k_s, df_v_p, df_v_s, fx_kv_p, fx_kv_s,
            fx_logf_p, fx_logf_s, mla_lat_p, mla_lat_s, mla_kr_p, mla_kr_s)
```

```python
import functools
import math

import numpy as np
import jax
import jax.numpy as jnp
from jax import lax
from jax.experimental import pallas as pl
from jax.experimental.pallas import tpu as pltpu

F32 = jnp.float32
BF16 = jnp.bfloat16

N_META = 16
ROPE_THETA = 10000.0
EPS = 1e-6
NEG = -1e30
FX_FORGET_BIAS = 5.0
N_BRANCH = 4
N_GROUPS = 4
EXPERTS_PER_GROUP = 4

LANES = 128
SUBLANES = 8
VMEM_LIMIT = 56 * 1024 * 1024

TQ = 256
TM_MERGE = 256
TM_MOE = 512
PAGES_PER_STEP = 4


def _cparams(sem):
    return pltpu.CompilerParams(dimension_semantics=sem, vmem_limit_bytes=VMEM_LIMIT)


def _dot(a, b):
    return jnp.dot(a, b, preferred_element_type=F32)


def _dot_nt(a, b):
    return lax.dot_general(a, b, (((1,), (1,)), ((), ())), preferred_element_type=F32)


def _split2(x):
    hi = x.astype(BF16)
    lo = (x - hi.astype(F32)).astype(BF16)
    return hi, lo


def _dot_hilo(x, w):
    hi, lo = _split2(x)
    return _dot(hi, w) + _dot(lo, w)


def _split3(x):
    a = x.astype(BF16)
    r = x - a.astype(F32)
    b = r.astype(BF16)
    c = (r - b.astype(F32)).astype(BF16)
    return a, b, c


def _dot_3way(x, w):
    a, b, c = _split3(x)
    return _dot(a, w) + _dot(b, w) + _dot(c, w)


def _log_sigmoid(z):
    return jnp.minimum(z, 0.0) - jnp.log1p(jnp.exp(-jnp.abs(z)))


def _lane_iota(shape):
    return lax.broadcasted_iota(jnp.int32, shape, len(shape) - 1)


def _swap_halves(y, seg):
    half = seg // 2
    lane = _lane_iota(y.shape)
    fwd = pltpu.roll(y, LANES - half, 1)
    bwd = pltpu.roll(y, half, 1)
    return jnp.where((lane % seg) < half, fwd, bwd)


def _seg_ones(n_rows, row_seg, n_cols, col_seg):
    r = np.arange(n_rows)[:, None] // row_seg
    c = np.arange(n_cols)[None, :] // col_seg
    return jnp.asarray((r == c).astype(np.float32), dtype=BF16)


def _mla_head_matrix():
    head = np.concatenate([np.arange(256) // 64, np.arange(128) // 32])
    return jnp.asarray((head[:, None] == head[None, :]).astype(np.float32), dtype=BF16)


def _proj_kernel(x_ref, tab_ref, gmix_ref, w_ref, g64_ref, fb_ref, gcq_ref, wuq_ref, gmq_ref,
                 glat_ref, wukv_ref, gmk_ref, s64_ref, s256_ref, m96_ref, snr_ref,
                 sbq_ref, sbkv_ref, dfq_ref, dfk_ref, dfv_ref, fxq_ref, fxkv_ref, misc_ref,
                 mq_ref, lat_ref, mk_ref, mv_ref):
    x = x_ref[...]
    h = x * lax.rsqrt(jnp.mean(x * x, axis=-1, keepdims=True) + EPS) * gmix_ref[...]
    hb = h.astype(BF16)
    tab = tab_ref[...]
    cos64, sin64 = tab[:, 0:128], tab[:, 128:256]
    cos32, sin32 = tab[:, 256:384], tab[:, 384:512]
    s64 = s64_ref[...]
    g64 = g64_ref[...]

    def proj(a, b):
        return _dot(hb, w_ref[:, a:b])

    def segnorm64(y, g):
        ms = _dot_hilo(y * y, s64) * (1.0 / 64.0)
        return y * lax.rsqrt(ms + EPS) * g

    def rope64(y):
        return y * cos64 + _swap_halves(y, 64) * sin64

    sbq_ref[...] = proj(0, 256)
    sbkv_ref[...] = proj(256, 512)
    for c in range(4):
        y = proj(512 + 128 * c, 640 + 128 * c)
        dfq_ref[:, 128 * c:128 * (c + 1)] = rope64(segnorm64(y, g64[0:1]))
    for c in range(2):
        y = proj(1024 + 128 * c, 1152 + 128 * c)
        dfk_ref[:, 128 * c:128 * (c + 1)] = rope64(segnorm64(y, g64[1:2]))
    dfv_ref[...] = proj(1280, 1536)
    for c in range(2):
        y = proj(1536 + 128 * c, 1664 + 128 * c)
        fxq_ref[:, 128 * c:128 * (c + 1)] = segnorm64(y, g64[2:3])
    fxkv_ref[:, 0:128] = segnorm64(proj(1792, 1920), g64[3:4])
    fxkv_ref[:, 128:256] = proj(1920, 2048)

    ym = proj(2432, 2560)
    lane = _lane_iota(ym.shape)
    kr = ym * cos32 + _swap_halves(ym, 32) * sin32
    logf = _log_sigmoid(ym + fb_ref[...])
    misc = jnp.where(lane < 32, kr, jnp.where(lane < 36, logf, 0.0))
    misc_ref[...] = misc

    cq = proj(2048, 2304)
    cq = cq * lax.rsqrt(jnp.mean(cq * cq, axis=-1, keepdims=True) + EPS) * gcq_ref[...]
    qf = _dot(cq.astype(BF16), wuq_ref[...])
    qr = qf[:, 256:384]
    qr = qr * cos32 + _swap_halves(qr, 32) * sin32
    qf = jnp.concatenate([qf[:, 0:256], qr], axis=1)
    ms = _dot_hilo(qf * qf, m96_ref[...]) * (1.0 / 96.0)
    mq_ref[...] = qf * lax.rsqrt(ms + EPS) * gmq_ref[...]

    ckv = proj(2304, 2432)
    lat = ckv * lax.rsqrt(jnp.mean(ckv * ckv, axis=-1, keepdims=True) + EPS) * glat_ref[...]
    lat_ref[...] = lat

    kv = _dot(lat.astype(BF16), wukv_ref[...])
    kn = kv[:, 0:256]
    mv_ref[...] = kv[:, 256:512]
    krm = jnp.where(lane < 32, kr, 0.0)
    n2 = jnp.sum(krm * krm, axis=-1, keepdims=True)
    kn2h, kn2l = _split2(kn * kn)
    s256 = s256_ref[...]
    n1 = _dot(kn2h, s256) + _dot(kn2l, s256)
    n1r = _dot(kn2h, snr_ref[...]) + _dot(kn2l, snr_ref[...])
    gmk = gmk_ref[...]
    mk_ref[:, 0:256] = kn * lax.rsqrt((n1 + n2) * (1.0 / 96.0) + EPS) * gmk[:, 0:256]
    krt = krm + pltpu.roll(krm, 32, 1) + pltpu.roll(krm, 64, 1) + pltpu.roll(krm, 96, 1)
    mk_ref[:, 256:384] = krt * lax.rsqrt((n1r + n2) * (1.0 / 96.0) + EPS) * gmk[:, 256:384]


def _proj_call(x, tab, lw, tm):
    B, Lp, D = x.shape
    nl = Lp // tm
    widths = (256, 256, 512, 256, 256, 256, 256, 128, 384, 128, 384, 256)

    def tok(w):
        return pl.BlockSpec((None, tm, w), lambda b, i: (b, i, 0))

    def full(a):
        return pl.BlockSpec(a.shape, lambda b, i: (0,) * a.ndim)

    consts = (lw['gmix'], lw['w_in'], lw['g64'], lw['fb'], lw['gcq'], lw['wuq'], lw['gmq'], lw['glat'],
              lw['wukv'], lw['gmk'], lw['s64'], lw['s256'], lw['m96'], lw['snr'])
    return pl.pallas_call(
        _proj_kernel,
        grid=(B, nl),
        in_specs=[tok(D), pl.BlockSpec((tm, 512), lambda b, i: (i, 0))] + [full(a) for a in consts],
        out_specs=[tok(w) for w in widths],
        out_shape=[jax.ShapeDtypeStruct((B, Lp, w), F32) for w in widths],
        compiler_params=_cparams(("parallel", "parallel")),
        name="proj",
    )(x, tab, *consts)


def _flash_kernel(*refs, G, tq, scale, use_bias):
    if use_bias:
        q_ref, k_ref, v_ref, cq_ref, ck_ref, o_ref, m_sc, l_sc, acc_sc = refs
    else:
        q_ref, k_ref, v_ref, o_ref, m_sc, l_sc, acc_sc = refs
    i = pl.program_id(2)
    dq = q_ref.shape[-1]
    dv = v_ref.shape[-1]
    rows = G * tq
    q = q_ref[...].reshape(rows, dq).astype(BF16)
    m_sc[...] = jnp.full((rows, 1), NEG, F32)
    l_sc[...] = jnp.zeros((rows, 1), F32)
    acc_sc[...] = jnp.zeros((rows, dv), F32)

    def step(j, masked):
        start = pl.multiple_of(j * tq, tq)
        k = k_ref[pl.ds(start, tq), :].astype(BF16)
        v = v_ref[pl.ds(start, tq), :].astype(BF16)
        s = _dot_nt(q, k) * scale
        if use_bias:
            ck = ck_ref[:, :, pl.ds(start, tq)]
            bias = (cq_ref[...] - ck).reshape(rows, tq)
            s = s + bias
        if masked:
            qpos = lax.broadcasted_iota(jnp.int32, (G, tq, tq), 1).reshape(rows, tq)
            kpos = lax.broadcasted_iota(jnp.int32, (rows, tq), 1)
            s = jnp.where(kpos <= qpos, s, NEG)
        m_old = m_sc[...]
        m_new = jnp.maximum(m_old, jnp.max(s, axis=-1, keepdims=True))
        p = jnp.exp(s - m_new)
        alpha = jnp.exp(m_old - m_new)
        l_sc[...] = alpha * l_sc[...] + jnp.sum(p, axis=-1, keepdims=True)
        acc_sc[...] = alpha * acc_sc[...] + _dot(p.astype(BF16), v)
        m_sc[...] = m_new

    def body(j, c):
        step(j, False)
        return c

    lax.fori_loop(0, i, body, 0)
    step(i, True)
    o = acc_sc[...] / l_sc[...]
    o_ref[...] = o.reshape(G, tq, dv)


def _flash_call(q, k, v, scale, cq=None, ck=None):
    B, H, G, Lp, dq = q.shape
    Hv, dv = v.shape[1], v.shape[-1]
    nq = Lp // TQ
    use_bias = cq is not None
    in_specs = [
        pl.BlockSpec((None, None, G, TQ, dq), lambda b, h, i: (b, h, 0, i, 0)),
        pl.BlockSpec((None, None, Lp, dq), lambda b, h, i: (b, h, 0, 0)),
        pl.BlockSpec((None, None, Lp, dv), lambda b, h, i: (b, h % Hv, 0, 0)),
    ]
    args = [q, k, v]
    if use_bias:
        in_specs += [
            pl.BlockSpec((None, None, G, TQ, 1), lambda b, h, i: (b, h, 0, i, 0)),
            pl.BlockSpec((None, None, G, 1, Lp), lambda b, h, i: (b, h, 0, 0, 0)),
        ]
        args += [cq, ck]
    return pl.pallas_call(
        functools.partial(_flash_kernel, G=G, tq=TQ, scale=scale, use_bias=use_bias),
        grid=(B, H, nq),
        in_specs=in_specs,
        out_specs=pl.BlockSpec((None, None, G, TQ, dv), lambda b, h, i: (b, h, 0, i, 0)),
        out_shape=jax.ShapeDtypeStruct((B, H, G, Lp, dv), F32),
        scratch_shapes=[pltpu.VMEM((G * TQ, 1), F32), pltpu.VMEM((G * TQ, 1), F32),
                        pltpu.VMEM((G * TQ, dv), F32)],
        compiler_params=_cparams(("parallel", "parallel", "arbitrary")),
        name="flash_bias" if use_bias else "flash",
    )(*args)


def _sb_kernel(q_ref, k_ref, v_ref, u_ref, o_ref, r_sc, acc_sc, *, G, tq, scale):
    i = pl.program_id(2)
    dq = q_ref.shape[-1]
    rows = G * tq
    q = q_ref[...].reshape(rows, dq).astype(BF16)
    r_sc[...] = jnp.zeros((rows, 1), F32)
    acc_sc[...] = jnp.zeros((rows, dq), F32)
    u = u_ref[...]

    def step(j, masked):
        start = pl.multiple_of(j * tq, tq)
        k = k_ref[pl.ds(start, tq), :].astype(BF16)
        v = v_ref[pl.ds(start, tq), :].astype(BF16)
        z = _dot_nt(q, k) * scale
        ls = _log_sigmoid(z)
        lk = ls - z
        if masked:
            qpos = lax.broadcasted_iota(jnp.int32, (G, tq, tq), 1).reshape(rows, tq)
            kpos = lax.broadcasted_iota(jnp.int32, (rows, tq), 1)
            vis = kpos < qpos
            lk = jnp.where(vis, lk, 0.0)
        later = _dot_hilo(lk, u) + r_sc[...]
        w = jnp.exp(ls + later)
        if masked:
            w = jnp.where(vis, w, 0.0)
        acc_sc[...] += _dot(w.astype(BF16), v)
        r_sc[...] += jnp.sum(lk, axis=-1, keepdims=True)

    step(i, True)

    def body(t, c):
        step(i - 1 - t, False)
        return c

    lax.fori_loop(0, i, body, 0)
    o_ref[...] = acc_sc[...].reshape(G, tq, dq)


def _sb_call(q, k, v, scale):
    B, H, G, Lp, dq = q.shape
    nq = Lp // TQ
    u = jnp.asarray(np.tril(np.ones((TQ, TQ), np.float32), -1), dtype=BF16)
    return pl.pallas_call(
        functools.partial(_sb_kernel, G=G, tq=TQ, scale=scale),
        grid=(B, H, nq),
        in_specs=[
            pl.BlockSpec((None, None, G, TQ, dq), lambda b, h, i: (b, h, 0, i, 0)),
            pl.BlockSpec((None, None, Lp, dq), lambda b, h, i: (b, h, 0, 0)),
            pl.BlockSpec((None, None, Lp, dq), lambda b, h, i: (b, h, 0, 0)),
            pl.BlockSpec((TQ, TQ), lambda b, h, i: (0, 0)),
        ],
        out_specs=pl.BlockSpec((None, None, G, TQ, dq), lambda b, h, i: (b, h, 0, i, 0)),
        out_shape=jax.ShapeDtypeStruct((B, H, G, Lp, dq), F32),
        scratch_shapes=[pltpu.VMEM((G * TQ, 1), F32), pltpu.VMEM((G * TQ, dq), F32)],
        compiler_params=_cparams(("parallel", "parallel", "arbitrary")),
        name="sb",
    )(q, k, v, u)


def _cumsum_kernel(x_ref, u_ref, o_ref, *, blk):
    n = x_ref.shape[-1] // blk
    u = u_ref[...]
    carry = jnp.zeros((x_ref.shape[0], 1), F32)
    for j in range(n):
        x = x_ref[:, j * blk:(j + 1) * blk]
        c = _dot_3way(x, u) + carry
        o_ref[:, j * blk:(j + 1) * blk] = c
        carry = c[:, blk - 1:blk]


def _cumsum_call(x):
    R, Lp = x.shape
    blk = 256
    u = jnp.asarray(np.triu(np.ones((blk, blk), np.float32)), dtype=BF16)
    return pl.pallas_call(
        functools.partial(_cumsum_kernel, blk=blk),
        out_shape=jax.ShapeDtypeStruct((R, Lp), F32),
        name="logf_cumsum",
    )(x, u)


def _page_suffix_kernel(x_ref, m_ref, o_ref):
    o_ref[...] = _dot_3way(x_ref[...], m_ref[...])


def _page_suffix_call(logf_flat):
    P = logf_flat.shape[0]
    kk = np.arange(512) // 4
    hh = np.arange(512) % 4
    ho = np.arange(512) // 128
    ko = np.arange(512) % 128
    same = hh[:, None] == ho[None, :]
    m = np.concatenate([same & (kk[:, None] > ko[None, :]), same], axis=1).astype(np.float32)
    tp = 512
    return pl.pallas_call(
        _page_suffix_kernel,
        grid=(pl.cdiv(P, tp),),
        in_specs=[pl.BlockSpec((tp, 512), lambda i: (i, 0)), pl.BlockSpec((512, 1024), lambda i: (0, 0))],
        out_specs=pl.BlockSpec((tp, 1024), lambda i: (i, 0)),
        out_shape=jax.ShapeDtypeStruct((P, 1024), F32),
        compiler_params=_cparams(("parallel",)),
        name="page_suffix",
    )(logf_flat, jnp.asarray(m, dtype=BF16))


def _merge_kernel(x_ref, sb_ref, df0_ref, df1_ref, fx_ref, ml_ref, gmix_ref, lam_ref, gdf_ref,
                  wg_ref, wb_ref, wo_ref, o_ref, *, lam_init):
    x = x_ref[...]
    D = x.shape[-1]
    h = (x * lax.rsqrt(jnp.mean(x * x, axis=-1, keepdims=True) + EPS) * gmix_ref[...]).astype(BF16)
    lp = lam_ref[...]
    lam = (jnp.exp(jnp.sum(lp[0:1] * lp[1:2], axis=-1, keepdims=True))
           - jnp.exp(jnp.sum(lp[2:3] * lp[3:4], axis=-1, keepdims=True)) + lam_init)
    gdf = gdf_ref[...]
    dfs = []
    for c in range(4):
        a = df0_ref[:, 128 * c:128 * (c + 1)] - lam * df1_ref[:, 128 * c:128 * (c + 1)]
        a = a * lax.rsqrt(jnp.mean(a * a, axis=-1, keepdims=True) + EPS) * gdf
        dfs.append(a * (1.0 - lam_init))
    df = jnp.concatenate(dfs, axis=1)
    merged = jnp.zeros(x.shape, F32)
    off = 0
    for b, o in enumerate((sb_ref[...], df, fx_ref[...], ml_ref[...])):
        width = o.shape[-1]
        gate = jax.nn.sigmoid(_dot(h, wg_ref[:, b * D:(b + 1) * D]))
        merged = merged + gate * _dot(o.astype(BF16), wb_ref[off:off + width, :])
        off += width
    o_ref[...] = x + _dot(merged.astype(BF16), wo_ref[...])


def _merge_call(x, sb, df0, df1, fx, ml, lw, lam_init, tm):
    N, D = x.shape

    def tok(w):
        return pl.BlockSpec((tm, w), lambda i: (i, 0))

    def full(a):
        return pl.BlockSpec(a.shape, lambda i: (0,) * a.ndim)

    consts = (lw['gmix'], lw['lam'], lw['gdf'], lw['w_gate'], lw['w_branch'], lw['w_out'])
    return pl.pallas_call(
        functools.partial(_merge_kernel, lam_init=lam_init),
        grid=(N // tm,),
        in_specs=[tok(D), tok(256), tok(512), tok(512), tok(256), tok(256)] + [full(a) for a in consts],
        out_specs=tok(D),
        out_shape=jax.ShapeDtypeStruct((N, D), F32),
        compiler_params=_cparams(("parallel",)),
        name="merge",
    )(x, sb, df0, df1, fx, ml, *consts)


def _moe_kernel(x_ref, gffn_ref, wgr_ref, bgr_ref, wrt_ref, brt_ref, wgu_ref, wdn_ref, o_ref,
                h_sc, comb_sc, acc_sc):
    gi = pl.program_id(1)
    tm = x_ref.shape[0]
    hid = wdn_ref.shape[1]

    @pl.when(gi == 0)
    def _():
        x = x_ref[...]
        h = (x * lax.rsqrt(jnp.mean(x * x, axis=-1, keepdims=True) + EPS) * gffn_ref[...]).astype(BF16)
        h_sc[...] = h
        lane = _lane_iota((tm, LANES))
        big = jnp.int32(LANES)
        gl = jnp.where(lane < N_GROUPS, _dot(h, wgr_ref[...]) + bgr_ref[...], NEG)
        gmax = jnp.max(gl, axis=-1, keepdims=True)
        gidx = jnp.min(jnp.where(gl == gmax, lane, big), axis=-1, keepdims=True)
        g_w = 1.0 / jnp.sum(jnp.exp(gl - gmax), axis=-1, keepdims=True)
        sel = (lane // EXPERTS_PER_GROUP) == gidx
        el = jnp.where(sel, _dot(h, wrt_ref[...]) + brt_ref[...], NEG)
        emax = jnp.max(el, axis=-1, keepdims=True)
        pe = jnp.exp(el - emax)
        p = pe / jnp.sum(pe, axis=-1, keepdims=True)
        p = jnp.where(sel, p, -1.0)
        top1 = jnp.max(p, axis=-1, keepdims=True)
        i1 = jnp.min(jnp.where(p == top1, lane, big), axis=-1, keepdims=True)
        p2 = jnp.where(lane == i1, -1.0, p)
        top2 = jnp.max(p2, axis=-1, keepdims=True)
        i2 = jnp.min(jnp.where(p2 == top2, lane, big), axis=-1, keepdims=True)
        den = top1 + top2
        comb_sc[...] = (jnp.where(lane == i1, top1 / den * g_w, 0.0)
                        + jnp.where(lane == i2, top2 / den * g_w, 0.0))
        acc_sc[...] = jnp.zeros(acc_sc.shape, F32)

    h = h_sc[...]
    comb_hi, comb_lo = _split2(comb_sc[...])
    row = lax.broadcasted_iota(jnp.int32, (LANES, hid), 0)
    for e in range(EXPERTS_PER_GROUP):
        sel = jnp.where(row == gi * EXPERTS_PER_GROUP + e, 1.0, 0.0).astype(BF16)
        cb = _dot(comb_hi, sel) + _dot(comb_lo, sel)
        gu = _dot(h, wgu_ref[e])
        g = gu[:, :hid]
        act = g * jax.nn.sigmoid(g) * gu[:, hid:] * cb
        acc_sc[...] += _dot(act.astype(BF16), wdn_ref[e])

    @pl.when(gi == pl.num_programs(1) - 1)
    def _():
        o_ref[...] = x_ref[...] + acc_sc[...]


def _moe_call(x, lw, tm):
    N, D = x.shape
    hid = lw['w_down'].shape[1]
    epg = EXPERTS_PER_GROUP

    def full(a):
        return pl.BlockSpec(a.shape, lambda i, g: (0,) * a.ndim)

    return pl.pallas_call(
        _moe_kernel,
        grid=(N // tm, N_GROUPS),
        in_specs=[pl.BlockSpec((tm, D), lambda i, g: (i, 0)),
                  full(lw['gffn']), full(lw['w_group']), full(lw['b_group']),
                  full(lw['w_router']), full(lw['b_router']),
                  pl.BlockSpec((epg, D, 2 * hid), lambda i, g: (g, 0, 0)),
                  pl.BlockSpec((epg, hid, D), lambda i, g: (g, 0, 0))],
        out_specs=pl.BlockSpec((tm, D), lambda i, g: (i, 0)),
        out_shape=jax.ShapeDtypeStruct((N, D), F32),
        scratch_shapes=[pltpu.VMEM((tm, D), BF16), pltpu.VMEM((tm, LANES), F32), pltpu.VMEM((tm, D), F32)],
        compiler_params=_cparams(("parallel", "arbitrary")),
        name="moe",
    )(x, lw['gffn'], lw['w_group'], lw['b_group'], lw['w_router'], lw['b_router'],
      lw['w_gate_up'], lw['w_down'])


def _decode_kernel(pt_ref, qsb_ref, qdf_ref, qfx_ref, qmn_ref, qmr_ref,
                   ndfk_ref, ndfv_ref, nfxkv_ref, nlat_ref, nmisc_ref,
                   u_ref, wuk_ref, wuv_ref, gmk_ref, ind_ref, *rest, npp):
    del pt_ref
    pages = rest[:7 * npp]
    osb_ref, odf_ref, ofx_ref, oml_ref = rest[7 * npp:7 * npp + 4]
    (sb_acc, sb_car, df_m, df_l, df_acc, fx_m, fx_l, fx_acc, fx_car,
     ml_m, ml_l, ml_acc) = rest[7 * npp + 4:]
    c = pl.program_id(1)
    R = SUBLANES

    @pl.when(c == 0)
    def _():
        sb_acc[...] = jnp.zeros(sb_acc.shape, F32)
        sb_car[...] = jnp.zeros(sb_car.shape, F32)
        for m_sc, l_sc, a_sc in ((df_m, df_l, df_acc), (fx_m, fx_l, fx_acc), (ml_m, ml_l, ml_acc)):
            m_sc[...] = jnp.full(m_sc.shape, NEG, F32)
            l_sc[...] = jnp.zeros(l_sc.shape, F32)
            a_sc[...] = jnp.zeros(a_sc.shape, F32)
        nm = nmisc_ref[...]
        rowi = lax.broadcasted_iota(jnp.int32, (R, LANES), 0)
        lanei = lax.broadcasted_iota(jnp.int32, (R, LANES), 1)
        pick = jnp.where(lanei == 32 + (rowi % 4), jnp.broadcast_to(nm, (R, LANES)), 0.0)
        fx_car[...] = jnp.sum(pick, axis=-1, keepdims=True)

    qsb = qsb_ref[...].astype(BF16)
    qdf = qdf_ref[...].astype(BF16)
    qfx = qfx_ref[...].astype(BF16)
    gmk = gmk_ref[...]
    qmn = (qmn_ref[...] * gmk[:, 0:256]).astype(BF16)
    qmr = (qmr_ref[...] * gmk[:, 256:288]).astype(BF16)
    ind = ind_ref[...]
    ones_r = jnp.ones((R, 32), BF16)
    u = u_ref[...]
    wuk = wuk_ref[...]

    def softmax_update(m_sc, l_sc, a_sc, s, vb):
        m_old = m_sc[...]
        m_new = jnp.maximum(m_old, jnp.max(s, axis=-1, keepdims=True))
        p = jnp.exp(s - m_new)
        alpha = jnp.exp(m_old - m_new)
        l_sc[...] = alpha * l_sc[...] + jnp.sum(p, axis=-1, keepdims=True)
        a_sc[...] = alpha * a_sc[...] + _dot(p.astype(BF16), vb)
        m_sc[...] = m_new

    for pidx in reversed(range(npp)):
        sbkv_ref, dfk_ref, dfv_ref, fxkv_ref, lat_ref, kr_ref, fxb_ref = pages[7 * pidx:7 * pidx + 7]
        z = _dot_nt(qsb, sbkv_ref[:, 0:128].astype(BF16)) * 0.125
        ls = _log_sigmoid(z)
        lk = ls - z
        hi, lo = _split2(lk)
        suf = _dot(jnp.concatenate([hi, lo], axis=0), u)
        later = suf[0:R] + suf[R:2 * R] + sb_car[...]
        w = jnp.exp(ls + later)
        sb_acc[...] += _dot(w.astype(BF16), sbkv_ref[:, 128:256].astype(BF16))
        sb_car[...] += jnp.sum(lk, axis=-1, keepdims=True)
        s = _dot_nt(qdf, dfk_ref[...].astype(BF16)) * 0.125
        softmax_update(df_m, df_l, df_acc, s, dfv_ref[...].astype(BF16))
        fxb = fxb_ref[...]
        s = _dot_nt(qfx, fxkv_ref[:, 0:128].astype(BF16)) * 0.125 + fxb + fx_car[...]
        softmax_update(fx_m, fx_l, fx_acc, s, fxkv_ref[:, 128:256].astype(BF16))
        fx_car[...] += pltpu.roll(fxb, 4, 0)[:, 0:1]
        latb = lat_ref[...].astype(BF16)
        kn = _dot(latb, wuk)
        kr = kr_ref[...]
        a = _dot_nt(qmn, kn.astype(BF16)) + _dot_nt(qmr, kr.astype(BF16))
        n = _dot_nt(ind, (kn * kn).astype(BF16)) + _dot_nt(ones_r, (kr * kr).astype(BF16))
        s = a * lax.rsqrt(n * (1.0 / 96.0) + EPS) * (96.0 ** -0.5)
        softmax_update(ml_m, ml_l, ml_acc, s, latb)

    @pl.when(c == pl.num_programs(1) - 1)
    def _():
        def finish(m_sc, l_sc, a_sc, s_new, v_new):
            m_old = m_sc[...]
            m_new = jnp.maximum(m_old, s_new)
            p = jnp.exp(s_new - m_new)
            alpha = jnp.exp(m_old - m_new)
            return (alpha * a_sc[...] + p * v_new) / (alpha * l_sc[...] + p)

        osb_ref[...] = sb_acc[...]
        s_new = jnp.sum(qdf_ref[...] * ndfk_ref[...], axis=-1, keepdims=True) * 0.125
        odf_ref[...] = finish(df_m, df_l, df_acc, s_new, ndfv_ref[...])
        nfxkv = nfxkv_ref[...]
        s_new = jnp.sum(qfx_ref[...] * nfxkv[:, 0:128], axis=-1, keepdims=True) * 0.125
        ofx_ref[...] = finish(fx_m, fx_l, fx_acc, s_new, nfxkv[:, 128:256])
        nlat = nlat_ref[...]
        nlat8 = jnp.broadcast_to(nlat, (R, LANES))
        kn = _dot(nlat8.astype(BF16), wuk)
        krn = nmisc_ref[...][:, 0:32]
        indf = ind.astype(F32)
        a = (jnp.sum(qmn_ref[...] * gmk[:, 0:256] * kn, axis=-1, keepdims=True)
             + jnp.sum(qmr_ref[...] * gmk[:, 256:288] * krn, axis=-1, keepdims=True))
        n = (jnp.sum(indf * kn * kn, axis=-1, keepdims=True)
             + jnp.sum(krn * krn, axis=-1, keepdims=True))
        s_new = a * lax.rsqrt(n * (1.0 / 96.0) + EPS) * (96.0 ** -0.5)
        pl_ = finish(ml_m, ml_l, ml_acc, s_new, nlat8)
        oml_ref[...] = _dot(pl_.astype(BF16), wuv_ref[...])


def _decode_call(l, page_table, q, new, consts, caches):
    nb, n_pages = page_table.shape
    npp = PAGES_PER_STEP
    assert n_pages % npp == 0
    nsteps = n_pages // npp
    pt_flat = page_table.reshape(-1)

    def seq(a):
        return pl.BlockSpec((None,) + a.shape[1:], lambda b, c, pt: (b,) + (0,) * (a.ndim - 1))

    def full(a):
        return pl.BlockSpec(a.shape, lambda b, c, pt: (0,) * a.ndim)

    def page(a, p):
        def imap(b, c, pt):
            j = (nsteps - 1 - c) * npp + p
            return (l, pt[b * n_pages + j]) + (0,) * (a.ndim - 2)
        return pl.BlockSpec((None, None) + a.shape[2:], imap)

    in_specs = [seq(a) for a in q] + [seq(a) for a in new] + [full(a) for a in consts]
    args = list(q) + list(new) + list(consts)
    for p in range(npp):
        in_specs += [page(a, p) for a in caches]
        args += list(caches)
    out_widths = (128, 256, 128, 256)
    grid_spec = pltpu.PrefetchScalarGridSpec(
        num_scalar_prefetch=1,
        grid=(nb, nsteps),
        in_specs=in_specs,
        out_specs=[pl.BlockSpec((None, SUBLANES, w), lambda b, c, pt: (b, 0, 0)) for w in out_widths],
        scratch_shapes=[pltpu.VMEM((SUBLANES, 128), F32), pltpu.VMEM((SUBLANES, 1), F32),
                        pltpu.VMEM((SUBLANES, 1), F32), pltpu.VMEM((SUBLANES, 1), F32),
                        pltpu.VMEM((SUBLANES, 256), F32),
                        pltpu.VMEM((SUBLANES, 1), F32), pltpu.VMEM((SUBLANES, 1), F32),
                        pltpu.VMEM((SUBLANES, 128), F32), pltpu.VMEM((SUBLANES, 1), F32),
                        pltpu.VMEM((SUBLANES, 1), F32), pltpu.VMEM((SUBLANES, 1), F32),
                        pltpu.VMEM((SUBLANES, 128), F32)],
    )
    return pl.pallas_call(
        functools.partial(_decode_kernel, npp=npp),
        grid_spec=grid_spec,
        out_shape=[jax.ShapeDtypeStruct((nb, SUBLANES, w), F32) for w in out_widths],
        compiler_params=_cparams(("parallel", "arbitrary")),
        name="decode",
    )(pt_flat, *args)


def _layer_weights(l, P):
    D = P['w_in'].shape[1]
    w_in = P['w_in'][l]
    w_in_p = jnp.concatenate([w_in[:, :2048], w_in[:, 2052:2436], w_in[:, 2436:2468], w_in[:, 2048:2052],
                              jnp.zeros((D, 92), F32)], axis=1).astype(BF16)
    t2 = lambda g: jnp.tile(g, 2)
    g64 = jnp.stack([t2(P['df_q_norm'][l]), t2(P['df_k_norm'][l]), t2(P['fx_q_norm'][l]), t2(P['fx_k_norm'][l])])
    g64 = jnp.concatenate([g64, jnp.zeros((4, 128), F32)], axis=0)
    fb = jnp.zeros((1, 128), F32).at[0, 32:36].set(P['fx_forget_bias'][l])
    wuq = P['mla_w_uq'][l]
    nh = wuq.shape[1]
    wuq_p = jnp.concatenate([wuq[:, :, :64].reshape(-1, nh * 64), wuq[:, :, 64:].reshape(-1, nh * 32)], axis=1)
    gq = P['mla_q_norm'][l]
    gmq = jnp.concatenate([jnp.tile(gq[:64], nh), jnp.tile(gq[64:], nh)])[None]
    gk = P['mla_k_norm'][l]
    gmk = jnp.concatenate([jnp.tile(gk[:64], nh), jnp.tile(gk[64:], nh)])[None]
    wukv = P['mla_w_ukv'][l]
    wuk = wukv[:, :, :64].reshape(-1, nh * 64)
    wuv = wukv[:, :, 64:].reshape(-1, nh * 64)
    return dict(
        gmix=P['norm_mix'][l][None], w_in=w_in_p, g64=g64, fb=fb,
        gcq=P['mla_q_lora_norm'][l][None], wuq=wuq_p.astype(BF16), gmq=gmq,
        glat=P['mla_kv_lora_norm'][l][None], wukv=jnp.concatenate([wuk, wuv], axis=1).astype(BF16),
        wuk=wuk.astype(BF16), wuv=wuv.astype(BF16), gmk=gmk,
        s64=_seg_ones(128, 64, 128, 64), s256=_seg_ones(256, 64, 256, 64), m96=_mla_head_matrix(), snr=_seg_ones(256, 64, 128, 32),
        lam=P['df_lambda'][l], gdf=P['df_out_norm'][l][None],
        w_gate=P['w_gate'][l].astype(BF16), w_branch=P['w_branch'][l].astype(BF16),
        w_out=P['w_out'][l].astype(BF16),
        gffn=P['norm_ffn'][l][None],
        w_group=jnp.pad(P['moe_w_group'][l], ((0, 0), (0, LANES - N_GROUPS))).astype(BF16),
        b_group=jnp.pad(P['moe_b_group'][l], (0, LANES - N_GROUPS))[None],
        w_router=jnp.pad(P['moe_w_router'][l], ((0, 0), (0, LANES - N_GROUPS * EXPERTS_PER_GROUP))).astype(BF16),
        b_router=jnp.pad(P['moe_b_router'][l], (0, LANES - N_GROUPS * EXPERTS_PER_GROUP))[None],
        w_gate_up=P['moe_w_gate_up'][l].astype(BF16), w_down=P['moe_w_down'][l].astype(BF16),
    )


def _rope_table(pos):
    def cs(seg):
        half = seg // 2
        inv = ROPE_THETA ** (-jnp.arange(half, dtype=F32) / half)
        ang = pos.astype(F32)[:, None] * inv[None, :]
        c, s = jnp.cos(ang), jnp.sin(ang)
        reps = LANES // seg
        return jnp.tile(jnp.concatenate([c, c], axis=1), (1, reps)), jnp.tile(jnp.concatenate([-s, s], axis=1), (1, reps))
    c64, s64 = cs(64)
    c32, s32 = cs(32)
    return jnp.concatenate([c64, s64, c32, s32], axis=1)


def _rows(outs, B, L):
    sbq, sbkv, dfq, dfk, dfv, fxq, fxkv, misc, mq, lat, mk, mv = outs
    return (sbkv[:, :L].reshape(B, L, 2, 2, 64), dfk[:, :L].reshape(B, L, 2, 2, 64),
            dfv[:, :L].reshape(B, L, 2, 128), fxkv[:, :L].reshape(B, L, 2, 2, 64),
            misc[:, :L, 32:36], lat[:, :L], misc[:, :L, 0:32])


def _heads(a, n_outer, d):
    B, Lp, _ = a.shape
    return a.reshape(B, Lp, n_outer, d).transpose(0, 2, 1, 3)


def _unheads(o):
    B, H, G, Lp, d = o.shape
    return o.transpose(0, 3, 1, 2, 4).reshape(B * Lp, H * G * d)


def _prompt_layer(x, l, lw, tab, L):
    B, Lp, D = x.shape
    outs = _proj_call(x, tab, lw, TQ)
    sbq, sbkv, dfq, dfk, dfv, fxq, fxkv, misc, mq, lat, mk, mv = outs

    def qh(a, n_outer, g, d):
        return a.reshape(B, Lp, n_outer, g, d).transpose(0, 2, 3, 1, 4)

    sb_kv = _heads(sbkv, 4, 64)
    sb_o = _sb_call(qh(sbq, 2, 2, 64), sb_kv[:, 0:2], sb_kv[:, 2:4], 64 ** -0.5)
    df_o = _flash_call(qh(dfq, 4, 2, 64), _heads(dfk, 4, 64), _heads(dfv, 2, 128), 64 ** -0.5)
    logf = misc[:, :, 32:36].transpose(0, 2, 1).reshape(B * 4, Lp)
    cum = _cumsum_call(logf).reshape(B, 2, 2, Lp)
    fx_kv = _heads(fxkv, 4, 64)
    fx_o = _flash_call(qh(fxq, 2, 2, 64), fx_kv[:, 0:2], fx_kv[:, 2:4], 64 ** -0.5,
                       cq=cum[..., None], ck=cum[:, :, :, None, :])
    ml_q = jnp.concatenate([qh(mq[..., 0:256], 4, 1, 64), qh(mq[..., 256:384], 4, 1, 32)], axis=-1)
    ml_k = jnp.concatenate([_heads(mk[..., 0:256], 4, 64), _heads(mk[..., 256:384], 4, 32)], axis=-1)
    ml_o = _flash_call(ml_q, ml_k, _heads(mv, 4, 64), 96 ** -0.5)

    lam_init = 0.8 - 0.6 * math.exp(-0.3 * l)
    xf = _merge_call(x.reshape(B * Lp, D), _unheads(sb_o), _unheads(df_o[:, 0:2]), _unheads(df_o[:, 2:4]),
                     _unheads(fx_o), _unheads(ml_o), lw, lam_init, TM_MERGE)
    xf = _moe_call(xf, lw, TM_MOE)
    return xf.reshape(B, Lp, D), _rows(outs, B, L)


def _block_diag_rows(q, n_heads, g, d):
    N = q.shape[0]
    q = q.reshape(N, n_heads, g, d)
    eye = jnp.eye(n_heads, dtype=q.dtype)
    out = jnp.einsum('nhgd,hk->nhgkd', q, eye).reshape(N, n_heads * g, n_heads * d)
    return jnp.pad(out, ((0, 0), (0, SUBLANES - n_heads * g), (0, 0)))


def _sample_layer(xs, l, lw, tab, page_table, caches, ind):
    N, D = xs.shape
    outs = _proj_call(xs[None], tab, lw, N)
    sbq, sbkv, dfq, dfk, dfv, fxq, fxkv, misc, mq, lat, mk, mv = [o[0] for o in outs]
    qsb = _block_diag_rows(sbq, 2, 2, 64)
    qdf = _block_diag_rows(dfq, 4, 2, 64)
    qfx = _block_diag_rows(fxq, 2, 2, 64)
    qmn = _block_diag_rows(mq[:, 0:256], 4, 1, 64)
    qmr = jnp.pad(mq[:, 256:384].reshape(N, 4, 32), ((0, 0), (0, 4), (0, 0)))
    new = (dfk[:, None], dfv[:, None], fxkv[:, None], lat[:, None], misc[:, None])
    u = jnp.asarray(np.tril(np.ones((128, 128), np.float32), -1), dtype=BF16)
    consts = (u, lw['wuk'], lw['wuv'], lw['gmk'], ind)
    osb, odf, ofx, oml = _decode_call(l, page_table, (qsb, qdf, qfx, qmn, qmr), new, consts, caches)
    sb_o = jnp.concatenate([osb[:, r, (r // 2) * 64:(r // 2 + 1) * 64] for r in range(4)], axis=-1)
    df0 = jnp.concatenate([odf[:, r, (r // 2) * 128:(r // 2 + 1) * 128] for r in range(4)], axis=-1)
    df1 = jnp.concatenate([odf[:, 4 + r, (r // 2) * 128:(r // 2 + 1) * 128] for r in range(4)], axis=-1)
    fx_o = jnp.concatenate([ofx[:, r, (r // 2) * 64:(r // 2 + 1) * 64] for r in range(4)], axis=-1)
    ml_o = jnp.concatenate([oml[:, r, r * 64:(r + 1) * 64] for r in range(4)], axis=-1)
    lam_init = 0.8 - 0.6 * math.exp(-0.3 * l)
    xf = _merge_call(xs, sb_o, df0, df1, fx_o, ml_o, lw, lam_init, N)
    xf = _moe_call(xf, lw, N)
    return xf, _rows([o[None] for o in (sbq, sbkv, dfq, dfk, dfv, fxq, fxkv, misc, mq, lat, mk, mv)], 1, N)


def kernel(x_prompt, x_sample, cache_sb_kv, cache_df_k, cache_df_v, cache_fx_kv, cache_fx_logf, cache_mla_latent, cache_mla_krope, page_table, meta_tokens, norm_mix, norm_ffn, w_in, w_gate, w_branch, w_out, df_q_norm, df_k_norm, df_lambda, df_out_norm, fx_q_norm, fx_k_norm, fx_forget_bias, mla_q_lora_norm, mla_kv_lora_norm, mla_w_uq, mla_w_ukv, mla_q_norm, mla_k_norm, moe_w_group, moe_b_group, moe_w_router, moe_b_router, moe_w_gate_up, moe_w_down):
    P = dict(norm_mix=norm_mix, norm_ffn=norm_ffn, w_in=w_in, w_gate=w_gate, w_branch=w_branch, w_out=w_out,
             df_q_norm=df_q_norm, df_k_norm=df_k_norm, df_lambda=df_lambda, df_out_norm=df_out_norm,
             fx_q_norm=fx_q_norm, fx_k_norm=fx_k_norm, fx_forget_bias=fx_forget_bias,
             mla_q_lora_norm=mla_q_lora_norm, mla_kv_lora_norm=mla_kv_lora_norm, mla_w_uq=mla_w_uq,
             mla_w_ukv=mla_w_ukv, mla_q_norm=mla_q_norm, mla_k_norm=mla_k_norm, moe_w_group=moe_w_group,
             moe_b_group=moe_b_group, moe_w_router=moe_w_router, moe_b_router=moe_b_router,
             moe_w_gate_up=moe_w_gate_up, moe_w_down=moe_w_down)
    depth = w_in.shape[0]
    B, S, D = x_prompt.shape
    L = S + N_META
    Lp = -(-L // TQ) * TQ
    lws = [_layer_weights(l, P) for l in range(depth)]

    meta = jnp.broadcast_to(meta_tokens[None].astype(x_prompt.dtype), (B, N_META, D))
    x = jnp.concatenate([meta, x_prompt, jnp.zeros((B, Lp - L, D), x_prompt.dtype)], axis=1)
    tab_p = _rope_table(jnp.arange(Lp, dtype=jnp.int32))
    rows_p = []
    for l in range(depth):
        x, rows = _prompt_layer(x, l, lws[l], tab_p, L)
        rows_p.append(rows)
    y_prompt = x[:, N_META:L]

    nb, dec_seq, _ = x_sample.shape
    assert dec_seq == 1
    n_pool, page = cache_sb_kv.shape[1], cache_sb_kv.shape[2]
    past_len = page_table.shape[1] * page
    tab_s = _rope_table(jnp.full((nb,), past_len, dtype=jnp.int32))
    fxb = _page_suffix_call(cache_fx_logf.reshape(depth * n_pool, page * 4)).reshape(depth, n_pool, SUBLANES, LANES)
    caches = (cache_sb_kv.reshape(depth, n_pool, page, 256), cache_df_k.reshape(depth, n_pool, page, 256),
              cache_df_v.reshape(depth, n_pool, page, 256), cache_fx_kv.reshape(depth, n_pool, page, 256),
              cache_mla_latent, cache_mla_krope, fxb)
    ind = _seg_ones(SUBLANES, 1, 256, 64)
    xs = x_sample.reshape(nb, D)
    rows_s = []
    for l in range(depth):
        xs, rows = _sample_layer(xs, l, lws[l], tab_s, page_table, caches, ind)
        rows_s.append(tuple(r.reshape((nb, 1) + r.shape[2:]) for r in rows))
    y_sample = xs.reshape(nb, 1, D)

    def stack(rl):
        return tuple(jnp.stack([r[i] for r in rl], axis=0) for i in range(len(rl[0])))

    sb_kv_p, df_k_p, df_v_p, fx_kv_p, fx_logf_p, mla_lat_p, mla_kr_p = stack(rows_p)
    sb_kv_s, df_k_s, df_v_s, fx_kv_s, fx_logf_s, mla_lat_s, mla_kr_s = stack(rows_s)
    return (y_prompt, y_sample, sb_kv_p, sb_kv_s, df_k_p, df_k_s, df_v_p, df_v_s, fx_kv_p, fx_kv_s,
            fx_logf_p, fx_logf_s, mla_lat_p, mla_lat_s, mla_kr_p, mla_kr_s)
```

```python
import functools
import math

import numpy as np
import jax
import jax.numpy as jnp
from jax import lax
from jax.experimental import pallas as pl
from jax.experimental.pallas import tpu as pltpu

F32 = jnp.float32
BF16 = jnp.bfloat16

N_META = 16
ROPE_THETA = 10000.0
EPS = 1e-6
NEG = -1e30
FX_FORGET_BIAS = 5.0
N_BRANCH = 4
N_GROUPS = 4
EXPERTS_PER_GROUP = 4

LANES = 128
SUBLANES = 8
VMEM_LIMIT = 56 * 1024 * 1024

TQ = 256
ROW_CHUNK = 512
TM_MERGE = 256
TM_MOE = 512
PAGES_PER_STEP = 8


def _cparams(sem):
    return pltpu.CompilerParams(dimension_semantics=sem, vmem_limit_bytes=VMEM_LIMIT)


def _dot(a, b):
    return jnp.dot(a, b, preferred_element_type=F32)


def _dot_nt(a, b):
    return lax.dot_general(a, b, (((1,), (1,)), ((), ())), preferred_element_type=F32)


def _split2(x):
    hi = x.astype(BF16)
    lo = (x - hi.astype(F32)).astype(BF16)
    return hi, lo


def _dot_hilo(x, w):
    hi, lo = _split2(x)
    return _dot(hi, w) + _dot(lo, w)


def _split3(x):
    a = x.astype(BF16)
    r = x - a.astype(F32)
    b = r.astype(BF16)
    c = (r - b.astype(F32)).astype(BF16)
    return a, b, c


def _dot_3way(x, w):
    a, b, c = _split3(x)
    return _dot(a, w) + _dot(b, w) + _dot(c, w)


def _log_sigmoid(z):
    return jnp.minimum(z, 0.0) - jnp.log1p(jnp.exp(-jnp.abs(z)))


def _lane_iota(shape):
    return lax.broadcasted_iota(jnp.int32, shape, len(shape) - 1)


def _swap_halves(y, seg):
    half = seg // 2
    lane = _lane_iota(y.shape)
    fwd = pltpu.roll(y, LANES - half, 1)
    bwd = pltpu.roll(y, half, 1)
    return jnp.where((lane % seg) < half, fwd, bwd)


def _seg_ones(n_rows, row_seg, n_cols, col_seg):
    r = np.arange(n_rows)[:, None] // row_seg
    c = np.arange(n_cols)[None, :] // col_seg
    return jnp.asarray((r == c).astype(np.float32), dtype=BF16)


def _mla_head_matrix():
    head = np.concatenate([np.arange(256) // 64, np.arange(128) // 32])
    return jnp.asarray((head[:, None] == head[None, :]).astype(np.float32), dtype=BF16)


def _proj_kernel(x_ref, tab_ref, gmix_ref, w_ref, g64_ref, fb_ref, gcq_ref, wuq_ref, gmq_ref,
                 glat_ref, wukv_ref, gmk_ref, s64_ref, s256_ref, m96_ref, snr_ref,
                 sbq_ref, sbkv_ref, dfq_ref, dfk_ref, dfv_ref, fxq_ref, fxkv_ref, misc_ref,
                 mq_ref, lat_ref, mk_ref, mv_ref):
    x = x_ref[...]
    h = x * lax.rsqrt(jnp.mean(x * x, axis=-1, keepdims=True) + EPS) * gmix_ref[...]
    hb = h.astype(BF16)
    tab = tab_ref[...]
    cos64, sin64 = tab[:, 0:128], tab[:, 128:256]
    cos32, sin32 = tab[:, 256:384], tab[:, 384:512]
    s64 = s64_ref[...]
    g64 = g64_ref[...]

    def proj(a, b):
        return _dot(hb, w_ref[:, a:b])

    def segnorm64(y, g):
        ms = _dot_hilo(y * y, s64) * (1.0 / 64.0)
        return y * lax.rsqrt(ms + EPS) * g

    def rope64(y):
        return y * cos64 + _swap_halves(y, 64) * sin64

    sbq_ref[...] = proj(0, 256)
    sbkv_ref[...] = proj(256, 512)
    for c in range(4):
        y = proj(512 + 128 * c, 640 + 128 * c)
        dfq_ref[:, 128 * c:128 * (c + 1)] = rope64(segnorm64(y, g64[0:1]))
    for c in range(2):
        y = proj(1024 + 128 * c, 1152 + 128 * c)
        dfk_ref[:, 128 * c:128 * (c + 1)] = rope64(segnorm64(y, g64[1:2]))
    dfv_ref[...] = proj(1280, 1536)
    for c in range(2):
        y = proj(1536 + 128 * c, 1664 + 128 * c)
        fxq_ref[:, 128 * c:128 * (c + 1)] = segnorm64(y, g64[2:3])
    fxkv_ref[:, 0:128] = segnorm64(proj(1792, 1920), g64[3:4])
    fxkv_ref[:, 128:256] = proj(1920, 2048)

    ym = proj(2432, 2560)
    lane = _lane_iota(ym.shape)
    kr = ym * cos32 + _swap_halves(ym, 32) * sin32
    logf = _log_sigmoid(ym + fb_ref[...])
    misc = jnp.where(lane < 32, kr, jnp.where(lane < 36, logf, 0.0))
    misc_ref[...] = misc

    cq = proj(2048, 2304)
    cq = cq * lax.rsqrt(jnp.mean(cq * cq, axis=-1, keepdims=True) + EPS) * gcq_ref[...]
    qf = _dot(cq.astype(BF16), wuq_ref[...])
    qr = qf[:, 256:384]
    qr = qr * cos32 + _swap_halves(qr, 32) * sin32
    qf = jnp.concatenate([qf[:, 0:256], qr], axis=1)
    ms = _dot_hilo(qf * qf, m96_ref[...]) * (1.0 / 96.0)
    mq_ref[...] = qf * lax.rsqrt(ms + EPS) * gmq_ref[...]

    ckv = proj(2304, 2432)
    lat = ckv * lax.rsqrt(jnp.mean(ckv * ckv, axis=-1, keepdims=True) + EPS) * glat_ref[...]
    lat_ref[...] = lat

    kv = _dot(lat.astype(BF16), wukv_ref[...])
    kn = kv[:, 0:256]
    mv_ref[...] = kv[:, 256:512]
    krm = jnp.where(lane < 32, kr, 0.0)
    n2 = jnp.sum(krm * krm, axis=-1, keepdims=True)
    kn2h, kn2l = _split2(kn * kn)
    s256 = s256_ref[...]
    n1 = _dot(kn2h, s256) + _dot(kn2l, s256)
    n1r = _dot(kn2h, snr_ref[...]) + _dot(kn2l, snr_ref[...])
    gmk = gmk_ref[...]
    mk_ref[:, 0:256] = kn * lax.rsqrt((n1 + n2) * (1.0 / 96.0) + EPS) * gmk[:, 0:256]
    krt = krm + pltpu.roll(krm, 32, 1) + pltpu.roll(krm, 64, 1) + pltpu.roll(krm, 96, 1)
    mk_ref[:, 256:384] = krt * lax.rsqrt((n1r + n2) * (1.0 / 96.0) + EPS) * gmk[:, 256:384]


def _proj_call(x, tab, lw, tm):
    B, Lp, D = x.shape
    nl = Lp // tm
    widths = (256, 256, 512, 256, 256, 256, 256, 128, 384, 128, 384, 256)

    def tok(w):
        return pl.BlockSpec((None, tm, w), lambda b, i: (b, i, 0))

    def full(a):
        return pl.BlockSpec(a.shape, lambda b, i: (0,) * a.ndim)

    consts = (lw['gmix'], lw['w_in'], lw['g64'], lw['fb'], lw['gcq'], lw['wuq'], lw['gmq'], lw['glat'],
              lw['wukv'], lw['gmk'], lw['s64'], lw['s256'], lw['m96'], lw['snr'])
    return pl.pallas_call(
        _proj_kernel,
        grid=(B, nl),
        in_specs=[tok(D), pl.BlockSpec((tm, 512), lambda b, i: (i, 0))] + [full(a) for a in consts],
        out_specs=[tok(w) for w in widths],
        out_shape=[jax.ShapeDtypeStruct((B, Lp, w), F32) for w in widths],
        compiler_params=_cparams(("parallel", "parallel")),
        name="proj",
    )(x, tab, *consts)


def _lane_tile(x, width):
    return x if width == LANES else jnp.concatenate([x] * (width // LANES), axis=1)


def _q_rows(q_ref, r0, ch):
    G, tq, d = q_ref.shape
    if ch <= tq:
        g, q0 = divmod(r0, tq)
        return q_ref[g, q0:q0 + ch, :]
    return q_ref[r0 // tq:(r0 + ch) // tq].reshape(ch, d)


def _rows_of_heads(per_head, r0, ch, tq):
    if ch <= tq:
        return per_head[r0 // tq]
    n = per_head[0].shape[-1]
    return jnp.concatenate([jnp.broadcast_to(per_head[g], (tq, n))
                            for g in range(r0 // tq, (r0 + ch) // tq)], axis=0)


def _flash_kernel(*refs, G, tq, scale, use_bias):
    if use_bias:
        q_ref, k_ref, v_ref, ck_ref, o_ref, m_sc, l_sc, acc_sc = refs
    else:
        q_ref, k_ref, v_ref, o_ref, m_sc, l_sc, acc_sc = refs
    i = pl.program_id(2)
    dv = v_ref.shape[-1]
    rows = G * tq
    ch = min(ROW_CHUNK, rows)
    m_sc[...] = jnp.full((rows, LANES), NEG, F32)
    l_sc[...] = jnp.zeros((rows, LANES), F32)
    acc_sc[...] = jnp.zeros((rows, dv), F32)
    if use_bias:
        c0 = ck_ref[:, :, pl.ds(pl.multiple_of(i * tq, tq), LANES)][:, :, 0:1]

    def step(j, masked):
        start = pl.multiple_of(j * tq, tq)
        k = k_ref[pl.ds(start, tq), :]
        v = v_ref[pl.ds(start, tq), :]
        if use_bias:
            bias = c0 - ck_ref[:, :, pl.ds(start, tq)]
        for r0 in range(0, rows, ch):
            s = _dot_nt(_q_rows(q_ref, r0, ch), k) * scale
            if use_bias:
                s = s + _rows_of_heads([bias[g] for g in range(G)], r0, ch, tq)
            if masked:
                qpos = (r0 + lax.broadcasted_iota(jnp.int32, (ch, tq), 0)) % tq
                kpos = lax.broadcasted_iota(jnp.int32, (ch, tq), 1)
                s = jnp.where(kpos <= qpos, s, NEG)
            m_old = m_sc[r0:r0 + ch, :]
            m_new = jnp.maximum(m_old, jnp.max(s, axis=-1, keepdims=True))
            p = jnp.exp(s - _lane_tile(m_new, tq))
            alpha = jnp.exp(m_old - m_new)
            l_sc[r0:r0 + ch, :] = alpha * l_sc[r0:r0 + ch, :] + jnp.sum(p, axis=-1, keepdims=True)
            acc_sc[r0:r0 + ch, :] = alpha[:, 0:dv] * acc_sc[r0:r0 + ch, :] + _dot(p.astype(BF16), v)
            m_sc[r0:r0 + ch, :] = m_new

    def body(j, c):
        step(j, False)
        return c

    lax.fori_loop(0, i, body, 0)
    step(i, True)
    o = acc_sc[...] / l_sc[:, 0:dv]
    o_ref[...] = o.reshape(G, tq, dv)


def _flash_call(q, k, v, scale, ck=None):
    B, H, G, Lp, dq = q.shape
    Hv, dv = v.shape[1], v.shape[-1]
    nq = Lp // TQ
    use_bias = ck is not None
    in_specs = [
        pl.BlockSpec((None, None, G, TQ, dq), lambda b, h, i: (b, h, 0, i, 0)),
        pl.BlockSpec((None, None, Lp, dq), lambda b, h, i: (b, h, 0, 0)),
        pl.BlockSpec((None, None, Lp, dv), lambda b, h, i: (b, h % Hv, 0, 0)),
    ]
    args = [q, k, v]
    if use_bias:
        in_specs.append(pl.BlockSpec((None, None, G, 1, Lp), lambda b, h, i: (b, h, 0, 0, 0)))
        args.append(ck)
    return pl.pallas_call(
        functools.partial(_flash_kernel, G=G, tq=TQ, scale=scale, use_bias=use_bias),
        grid=(B, H, nq),
        in_specs=in_specs,
        out_specs=pl.BlockSpec((None, None, G, TQ, dv), lambda b, h, i: (b, h, 0, i, 0)),
        out_shape=jax.ShapeDtypeStruct((B, H, G, Lp, dv), F32),
        scratch_shapes=[pltpu.VMEM((G * TQ, LANES), F32), pltpu.VMEM((G * TQ, LANES), F32),
                        pltpu.VMEM((G * TQ, dv), F32)],
        compiler_params=_cparams(("parallel", "parallel", "arbitrary")),
        name="flash_bias" if use_bias else "flash",
    )(*args)


def _sb_kernel(q_ref, k_ref, v_ref, u_ref, o_ref, r_sc, acc_sc, *, G, tq, scale):
    i = pl.program_id(2)
    dq = q_ref.shape[-1]
    rows = G * tq
    ch = min(ROW_CHUNK, rows)
    r_sc[...] = jnp.zeros((rows, LANES), F32)
    acc_sc[...] = jnp.zeros((rows, dq), F32)

    def step(j, masked):
        start = pl.multiple_of(j * tq, tq)
        k = k_ref[pl.ds(start, tq), :]
        v = v_ref[pl.ds(start, tq), :]
        u = u_ref[...]
        for r0 in range(0, rows, ch):
            z = _dot_nt(_q_rows(q_ref, r0, ch), k) * scale
            ls = _log_sigmoid(z)
            lk = ls - z
            if masked:
                qpos = (r0 + lax.broadcasted_iota(jnp.int32, (ch, tq), 0)) % tq
                kpos = lax.broadcasted_iota(jnp.int32, (ch, tq), 1)
                vis = kpos < qpos
                lk = jnp.where(vis, lk, 0.0)
            r_old = r_sc[r0:r0 + ch, :]
            later = _dot_hilo(lk, u) + _lane_tile(r_old, tq)
            w = jnp.exp(ls + later)
            if masked:
                w = jnp.where(vis, w, 0.0)
            acc_sc[r0:r0 + ch, :] += _dot(w.astype(BF16), v)
            r_sc[r0:r0 + ch, :] = r_old + jnp.sum(lk, axis=-1, keepdims=True)

    step(i, True)

    def body(t, c):
        step(i - 1 - t, False)
        return c

    lax.fori_loop(0, i, body, 0)
    o_ref[...] = acc_sc[...].reshape(G, tq, dq)


def _sb_call(q, k, v, scale):
    B, H, G, Lp, dq = q.shape
    nq = Lp // TQ
    u = jnp.asarray(np.tril(np.ones((TQ, TQ), np.float32), -1), dtype=BF16)
    return pl.pallas_call(
        functools.partial(_sb_kernel, G=G, tq=TQ, scale=scale),
        grid=(B, H, nq),
        in_specs=[
            pl.BlockSpec((None, None, G, TQ, dq), lambda b, h, i: (b, h, 0, i, 0)),
            pl.BlockSpec((None, None, Lp, dq), lambda b, h, i: (b, h, 0, 0)),
            pl.BlockSpec((None, None, Lp, dq), lambda b, h, i: (b, h, 0, 0)),
            pl.BlockSpec((TQ, TQ), lambda b, h, i: (0, 0)),
        ],
        out_specs=pl.BlockSpec((None, None, G, TQ, dq), lambda b, h, i: (b, h, 0, i, 0)),
        out_shape=jax.ShapeDtypeStruct((B, H, G, Lp, dq), F32),
        scratch_shapes=[pltpu.VMEM((G * TQ, LANES), F32), pltpu.VMEM((G * TQ, dq), F32)],
        compiler_params=_cparams(("parallel", "parallel", "arbitrary")),
        name="sb",
    )(q, k, v, u)


def _cumsum_kernel(x_ref, u_ref, o_ref, *, blk):
    n = x_ref.shape[-1] // blk
    u = u_ref[...]
    carry = jnp.zeros((x_ref.shape[0], 1), F32)
    for j in range(n):
        x = x_ref[:, j * blk:(j + 1) * blk]
        c = _dot_3way(x, u) + carry
        o_ref[:, j * blk:(j + 1) * blk] = c
        carry = c[:, blk - 1:blk]


def _cumsum_call(x):
    R, Lp = x.shape
    blk = 256
    u = jnp.asarray(np.triu(np.ones((blk, blk), np.float32)), dtype=BF16)
    return pl.pallas_call(
        functools.partial(_cumsum_kernel, blk=blk),
        out_shape=jax.ShapeDtypeStruct((R, Lp), F32),
        name="logf_cumsum",
    )(x, u)


def _merge_kernel(x_ref, sb_ref, df0_ref, df1_ref, fx_ref, ml_ref, gmix_ref, lam_ref, gdf_ref,
                  wg_ref, wb_ref, wo_ref, o_ref, *, lam_init):
    x = x_ref[...]
    D = x.shape[-1]
    h = (x * lax.rsqrt(jnp.mean(x * x, axis=-1, keepdims=True) + EPS) * gmix_ref[...]).astype(BF16)
    lp = lam_ref[...]
    lam = (jnp.exp(jnp.sum(lp[0:1] * lp[1:2], axis=-1, keepdims=True))
           - jnp.exp(jnp.sum(lp[2:3] * lp[3:4], axis=-1, keepdims=True)) + lam_init)
    gdf = gdf_ref[...]
    dfs = []
    for c in range(4):
        a = df0_ref[:, 128 * c:128 * (c + 1)] - lam * df1_ref[:, 128 * c:128 * (c + 1)]
        a = a * lax.rsqrt(jnp.mean(a * a, axis=-1, keepdims=True) + EPS) * gdf
        dfs.append(a * (1.0 - lam_init))
    df = jnp.concatenate(dfs, axis=1)
    merged = jnp.zeros(x.shape, F32)
    off = 0
    for b, o in enumerate((sb_ref[...], df, fx_ref[...], ml_ref[...])):
        width = o.shape[-1]
        gate = jax.nn.sigmoid(_dot(h, wg_ref[:, b * D:(b + 1) * D]))
        merged = merged + gate * _dot(o.astype(BF16), wb_ref[off:off + width, :])
        off += width
    o_ref[...] = x + _dot(merged.astype(BF16), wo_ref[...])


def _merge_call(x, sb, df0, df1, fx, ml, lw, lam_init, tm):
    N, D = x.shape

    def tok(w):
        return pl.BlockSpec((tm, w), lambda i: (i, 0))

    def full(a):
        return pl.BlockSpec(a.shape, lambda i: (0,) * a.ndim)

    consts = (lw['gmix'], lw['lam'], lw['gdf'], lw['w_gate'], lw['w_branch'], lw['w_out'])
    return pl.pallas_call(
        functools.partial(_merge_kernel, lam_init=lam_init),
        grid=(N // tm,),
        in_specs=[tok(D), tok(256), tok(512), tok(512), tok(256), tok(256)] + [full(a) for a in consts],
        out_specs=tok(D),
        out_shape=jax.ShapeDtypeStruct((N, D), F32),
        compiler_params=_cparams(("parallel",)),
        name="merge",
    )(x, sb, df0, df1, fx, ml, *consts)


def _moe_kernel(x_ref, gffn_ref, wgr_ref, bgr_ref, wrt_ref, brt_ref, wgu_ref, wdn_ref, o_ref,
                h_sc, comb_sc, acc_sc):
    gi = pl.program_id(1)
    tm = x_ref.shape[0]
    hid = wdn_ref.shape[1]

    @pl.when(gi == 0)
    def _():
        x = x_ref[...]
        h, h_lo = _split2(x * lax.rsqrt(jnp.mean(x * x, axis=-1, keepdims=True) + EPS) * gffn_ref[...])
        h_sc[...] = h

        def logits(w_ref):
            return _dot(h, w_ref[0]) + (_dot(h_lo, w_ref[0]) + _dot(h, w_ref[1]))

        lane = _lane_iota((tm, LANES))
        big = jnp.int32(LANES)
        gl = jnp.where(lane < N_GROUPS, logits(wgr_ref) + bgr_ref[...], NEG)
        gmax = jnp.max(gl, axis=-1, keepdims=True)
        gidx = jnp.min(jnp.where(gl == gmax, lane, big), axis=-1, keepdims=True)
        g_w = 1.0 / jnp.sum(jnp.exp(gl - gmax), axis=-1, keepdims=True)
        sel = (lane // EXPERTS_PER_GROUP) == gidx
        el = jnp.where(sel, logits(wrt_ref) + brt_ref[...], NEG)
        emax = jnp.max(el, axis=-1, keepdims=True)
        pe = jnp.exp(el - emax)
        p = pe / jnp.sum(pe, axis=-1, keepdims=True)
        p = jnp.where(sel, p, -1.0)
        top1 = jnp.max(p, axis=-1, keepdims=True)
        i1 = jnp.min(jnp.where(p == top1, lane, big), axis=-1, keepdims=True)
        p2 = jnp.where(lane == i1, -1.0, p)
        top2 = jnp.max(p2, axis=-1, keepdims=True)
        i2 = jnp.min(jnp.where(p2 == top2, lane, big), axis=-1, keepdims=True)
        den = top1 + top2
        comb_sc[...] = (jnp.where(lane == i1, top1 / den * g_w, 0.0)
                        + jnp.where(lane == i2, top2 / den * g_w, 0.0))
        acc_sc[...] = jnp.zeros(acc_sc.shape, F32)

    h = h_sc[...]
    comb_hi, comb_lo = _split2(comb_sc[...])
    row = lax.broadcasted_iota(jnp.int32, (LANES, hid), 0)
    for e in range(EXPERTS_PER_GROUP):
        sel = jnp.where(row == gi * EXPERTS_PER_GROUP + e, 1.0, 0.0).astype(BF16)
        cb = _dot(comb_hi, sel) + _dot(comb_lo, sel)
        gu = _dot(h, wgu_ref[e])
        g = gu[:, :hid]
        act = g * jax.nn.sigmoid(g) * gu[:, hid:] * cb
        acc_sc[...] += _dot(act.astype(BF16), wdn_ref[e])

    @pl.when(gi == pl.num_programs(1) - 1)
    def _():
        o_ref[...] = x_ref[...] + acc_sc[...]


def _moe_call(x, lw, tm):
    N, D = x.shape
    hid = lw['w_down'].shape[1]
    epg = EXPERTS_PER_GROUP

    def full(a):
        return pl.BlockSpec(a.shape, lambda i, g: (0,) * a.ndim)

    return pl.pallas_call(
        _moe_kernel,
        grid=(N // tm, N_GROUPS),
        in_specs=[pl.BlockSpec((tm, D), lambda i, g: (i, 0)),
                  full(lw['gffn']), full(lw['w_group']), full(lw['b_group']),
                  full(lw['w_router']), full(lw['b_router']),
                  pl.BlockSpec((epg, D, 2 * hid), lambda i, g: (g, 0, 0)),
                  pl.BlockSpec((epg, hid, D), lambda i, g: (g, 0, 0))],
        out_specs=pl.BlockSpec((tm, D), lambda i, g: (i, 0)),
        out_shape=jax.ShapeDtypeStruct((N, D), F32),
        scratch_shapes=[pltpu.VMEM((tm, D), BF16), pltpu.VMEM((tm, LANES), F32), pltpu.VMEM((tm, D), F32)],
        compiler_params=_cparams(("parallel", "arbitrary")),
        name="moe",
    )(x, lw['gffn'], lw['w_group'], lw['b_group'], lw['w_router'], lw['b_router'],
      lw['w_gate_up'], lw['w_down'])


def _decode_kernel(pt_ref, qsb_ref, qdf_ref, qfx_ref, qmn_ref, qmr_ref,
                   ndfk_ref, ndfv_ref, nfxkv_ref, nlat_ref, nmisc_ref,
                   u_ref, wuk_ref, wuv_ref, gmk_ref, ind_ref, *rest, npp, n_pages):
    pages = rest[:7 * npp]
    osb_ref, odf_ref, ofx_ref, oml_ref = rest[7 * npp:7 * npp + 4]
    (sb_acc, sb_car, df_m, df_l, df_acc, fx_m, fx_l, fx_acc, fx_car,
     ml_m, ml_l, ml_acc) = rest[7 * npp + 4:]
    b = pl.program_id(0)
    c = pl.program_id(1)
    nsteps = pl.num_programs(1)
    R = SUBLANES

    @pl.when(c == 0)
    def _():
        sb_acc[...] = jnp.zeros(sb_acc.shape, F32)
        sb_car[...] = jnp.zeros(sb_car.shape, F32)
        for m_sc, l_sc, a_sc in ((df_m, df_l, df_acc), (fx_m, fx_l, fx_acc), (ml_m, ml_l, ml_acc)):
            m_sc[...] = jnp.full(m_sc.shape, NEG, F32)
            l_sc[...] = jnp.zeros(l_sc.shape, F32)
            a_sc[...] = jnp.zeros(a_sc.shape, F32)
        nm = nmisc_ref[...]
        rowi = lax.broadcasted_iota(jnp.int32, (R, LANES), 0)
        lanei = lax.broadcasted_iota(jnp.int32, (R, LANES), 1)
        pick = jnp.where(lanei == 32 + (rowi % 4), jnp.broadcast_to(nm, (R, LANES)), 0.0)
        fx_car[...] = jnp.sum(pick, axis=-1, keepdims=True)

    qsb = qsb_ref[...].astype(BF16)
    qdf = qdf_ref[...].astype(BF16)
    qfx = qfx_ref[...].astype(BF16)
    gmk = gmk_ref[...]
    qmn = (qmn_ref[...] * gmk[:, 0:256]).astype(BF16)
    qmr = (qmr_ref[...] * gmk[:, 256:288]).astype(BF16)
    ind = ind_ref[...]
    ones_r = jnp.ones((R, 32), BF16)
    wuk = wuk_ref[...]

    def rowsum(x):
        return jnp.sum(x, axis=-1, keepdims=True)

    sb_ls, sb_lk, df_s, fx_s, ml_s, lf_rows, latb = [], [], [], [], [], [], []
    for p in range(npp):
        sbT, dfkT, _, fxT, lat_ref, krT_ref, lf_ref = pages[7 * p:7 * p + 7]
        z = _dot(qsb, sbT[0].reshape(128, 128).astype(BF16)) * 0.125
        ls = _log_sigmoid(z)
        sb_ls.append(ls)
        sb_lk.append(ls - z)
        df_s.append(_dot(qdf, dfkT[...].reshape(256, 128).astype(BF16)) * 0.125)
        fx_s.append(_dot(qfx, fxT[0].reshape(128, 128).astype(BF16)) * 0.125)
        lb = lat_ref[...].astype(BF16)
        latb.append(lb)
        kn = _dot(lb, wuk)
        kr = krT_ref[...]
        a = _dot_nt(qmn, kn.astype(BF16)) + _dot(qmr, kr.astype(BF16))
        n = _dot_nt(ind, (kn * kn).astype(BF16)) + _dot(ones_r, (kr * kr).astype(BF16))
        ml_s.append(a * lax.rsqrt(n * (1.0 / 96.0) + EPS) * (96.0 ** -0.5))
        j = (nsteps - 1 - c) * npp + p
        odd = (pt_ref[b * n_pages + j] % 2) == 1
        blk = lf_ref[...]
        lf_rows.append(jnp.where(odd, pltpu.roll(blk, 4, 0), blk))

    pieces = []
    for p in range(npp):
        pieces += list(_split2(sb_lk[p])) + list(_split3(lf_rows[p]))
    suf = _dot(jnp.concatenate(pieces, axis=0), u_ref[...])
    sb_suf = [suf[5 * R * p:5 * R * p + R] + suf[5 * R * p + R:5 * R * p + 2 * R] for p in range(npp)]
    fx_suf = [suf[5 * R * p + 2 * R:5 * R * p + 3 * R] + suf[5 * R * p + 3 * R:5 * R * p + 4 * R]
              + suf[5 * R * p + 4 * R:5 * R * p + 5 * R] for p in range(npp)]

    car = sb_car[...]
    acc = sb_acc[...]
    for p in reversed(range(npp)):
        w = jnp.exp(sb_ls[p] + sb_suf[p] + car)
        acc = acc + _dot_nt(w.astype(BF16), pages[7 * p][1].reshape(128, 128).astype(BF16))
        car = car + rowsum(sb_lk[p])
    sb_acc[...] = acc
    sb_car[...] = car

    car = fx_car[...]
    for p in reversed(range(npp)):
        fx_s[p] = fx_s[p] + fx_suf[p] + car
        car = car + rowsum(lf_rows[p])
    fx_car[...] = car

    def softmax_step(m_sc, l_sc, a_sc, s_list, pv):
        m_old = m_sc[...]
        smax = s_list[0]
        for s in s_list[1:]:
            smax = jnp.maximum(smax, s)
        m_new = jnp.maximum(m_old, jnp.max(smax, axis=-1, keepdims=True))
        alpha = jnp.exp(m_old - m_new)
        ps = [jnp.exp(s - m_new) for s in s_list]
        psum = ps[0]
        for x in ps[1:]:
            psum = psum + x
        l_sc[...] = alpha * l_sc[...] + rowsum(psum)
        acc = alpha * a_sc[...]
        for p in range(npp):
            acc = acc + pv(p, ps[p].astype(BF16))
        a_sc[...] = acc
        m_sc[...] = m_new

    def df_pv(p, pb):
        dfv = pages[7 * p + 2]
        v0 = dfv[pl.ds(0, 128, stride=2), :].astype(BF16)
        v1 = dfv[pl.ds(1, 128, stride=2), :].astype(BF16)
        return jnp.concatenate([_dot(pb, v0), _dot(pb, v1)], axis=1)

    softmax_step(df_m, df_l, df_acc, df_s, df_pv)
    softmax_step(fx_m, fx_l, fx_acc, fx_s,
                 lambda p, pb: _dot_nt(pb, pages[7 * p + 3][1].reshape(128, 128).astype(BF16)))
    softmax_step(ml_m, ml_l, ml_acc, ml_s, lambda p, pb: _dot(pb, latb[p]))

    @pl.when(c == nsteps - 1)
    def _():
        def finish(m_sc, l_sc, a_sc, s_new, v_new):
            m_old = m_sc[...]
            m_new = jnp.maximum(m_old, s_new)
            p = jnp.exp(s_new - m_new)
            alpha = jnp.exp(m_old - m_new)
            return (alpha * a_sc[...] + p * v_new) / (alpha * l_sc[...] + p)

        osb_ref[...] = sb_acc[...]
        s_new = jnp.sum(qdf_ref[...] * ndfk_ref[...], axis=-1, keepdims=True) * 0.125
        odf_ref[...] = finish(df_m, df_l, df_acc, s_new, ndfv_ref[...])
        nfxkv = nfxkv_ref[...]
        s_new = jnp.sum(qfx_ref[...] * nfxkv[:, 0:128], axis=-1, keepdims=True) * 0.125
        ofx_ref[...] = finish(fx_m, fx_l, fx_acc, s_new, nfxkv[:, 128:256])
        nlat = nlat_ref[...]
        nlat8 = jnp.broadcast_to(nlat, (R, LANES))
        kn = _dot(nlat8.astype(BF16), wuk)
        krn = nmisc_ref[...][:, 0:32]
        indf = ind.astype(F32)
        a = (jnp.sum(qmn_ref[...] * gmk[:, 0:256] * kn, axis=-1, keepdims=True)
             + jnp.sum(qmr_ref[...] * gmk[:, 256:288] * krn, axis=-1, keepdims=True))
        n = (jnp.sum(indf * kn * kn, axis=-1, keepdims=True)
             + jnp.sum(krn * krn, axis=-1, keepdims=True))
        s_new = a * lax.rsqrt(n * (1.0 / 96.0) + EPS) * (96.0 ** -0.5)
        pl_ = finish(ml_m, ml_l, ml_acc, s_new, nlat8)
        oml_ref[...] = _dot(pl_.astype(BF16), wuv_ref[...])


def _decode_call(l, page_table, q, new, consts, caches):
    nb, n_pages = page_table.shape
    npp = PAGES_PER_STEP
    assert n_pages % npp == 0
    nsteps = n_pages // npp
    pt_flat = page_table.reshape(-1)

    def seq(a):
        return pl.BlockSpec((None,) + a.shape[1:], lambda b, c, pt: (b,) + (0,) * (a.ndim - 1))

    def full(a):
        return pl.BlockSpec(a.shape, lambda b, c, pt: (0,) * a.ndim)

    def page(a, p, pages_per_block):
        def imap(b, c, pt):
            j = (nsteps - 1 - c) * npp + p
            return (l, pt[b * n_pages + j] // pages_per_block) + (0,) * (a.ndim - 2)
        return pl.BlockSpec((None, None) + a.shape[2:], imap)

    in_specs = [seq(a) for a in q] + [seq(a) for a in new] + [full(a) for a in consts]
    args = list(q) + list(new) + list(consts)
    for p in range(npp):
        in_specs += [page(a, p, 1) for a in caches[:6]] + [page(caches[6], p, 2)]
        args += list(caches)
    out_widths = (128, 256, 128, 256)
    grid_spec = pltpu.PrefetchScalarGridSpec(
        num_scalar_prefetch=1,
        grid=(nb, nsteps),
        in_specs=in_specs,
        out_specs=[pl.BlockSpec((None, SUBLANES, w), lambda b, c, pt: (b, 0, 0)) for w in out_widths],
        scratch_shapes=[pltpu.VMEM((SUBLANES, 128), F32), pltpu.VMEM((SUBLANES, 1), F32),
                        pltpu.VMEM((SUBLANES, 1), F32), pltpu.VMEM((SUBLANES, 1), F32),
                        pltpu.VMEM((SUBLANES, 256), F32),
                        pltpu.VMEM((SUBLANES, 1), F32), pltpu.VMEM((SUBLANES, 1), F32),
                        pltpu.VMEM((SUBLANES, 128), F32), pltpu.VMEM((SUBLANES, 1), F32),
                        pltpu.VMEM((SUBLANES, 1), F32), pltpu.VMEM((SUBLANES, 1), F32),
                        pltpu.VMEM((SUBLANES, 128), F32)],
    )
    return pl.pallas_call(
        functools.partial(_decode_kernel, npp=npp, n_pages=n_pages),
        grid_spec=grid_spec,
        out_shape=[jax.ShapeDtypeStruct((nb, SUBLANES, w), F32) for w in out_widths],
        compiler_params=_cparams(("parallel", "arbitrary")),
        name="decode",
    )(pt_flat, *args)


def _hi_lo(w):
    return jnp.stack(_split2(w))


def _layer_weights(l, P):
    D = P['w_in'].shape[1]
    w_in = P['w_in'][l]
    w_in_p = jnp.concatenate([w_in[:, :2048], w_in[:, 2052:2436], w_in[:, 2436:2468], w_in[:, 2048:2052],
                              jnp.zeros((D, 92), F32)], axis=1).astype(BF16)
    t2 = lambda g: jnp.tile(g, 2)
    g64 = jnp.stack([t2(P['df_q_norm'][l]), t2(P['df_k_norm'][l]), t2(P['fx_q_norm'][l]), t2(P['fx_k_norm'][l])])
    g64 = jnp.concatenate([g64, jnp.zeros((4, 128), F32)], axis=0)
    fb = jnp.zeros((1, 128), F32).at[0, 32:36].set(P['fx_forget_bias'][l])
    wuq = P['mla_w_uq'][l]
    nh = wuq.shape[1]
    wuq_p = jnp.concatenate([wuq[:, :, :64].reshape(-1, nh * 64), wuq[:, :, 64:].reshape(-1, nh * 32)], axis=1)
    gq = P['mla_q_norm'][l]
    gmq = jnp.concatenate([jnp.tile(gq[:64], nh), jnp.tile(gq[64:], nh)])[None]
    gk = P['mla_k_norm'][l]
    gmk = jnp.concatenate([jnp.tile(gk[:64], nh), jnp.tile(gk[64:], nh)])[None]
    wukv = P['mla_w_ukv'][l]
    wuk = wukv[:, :, :64].reshape(-1, nh * 64)
    wuv = wukv[:, :, 64:].reshape(-1, nh * 64)
    return dict(
        gmix=P['norm_mix'][l][None], w_in=w_in_p, g64=g64, fb=fb,
        gcq=P['mla_q_lora_norm'][l][None], wuq=wuq_p.astype(BF16), gmq=gmq,
        glat=P['mla_kv_lora_norm'][l][None], wukv=jnp.concatenate([wuk, wuv], axis=1).astype(BF16),
        wuk=wuk.astype(BF16), wuv=wuv.astype(BF16), gmk=gmk,
        s64=_seg_ones(128, 64, 128, 64), s256=_seg_ones(256, 64, 256, 64), m96=_mla_head_matrix(), snr=_seg_ones(256, 64, 128, 32),
        lam=P['df_lambda'][l], gdf=P['df_out_norm'][l][None],
        w_gate=P['w_gate'][l].astype(BF16), w_branch=P['w_branch'][l].astype(BF16),
        w_out=P['w_out'][l].astype(BF16),
        gffn=P['norm_ffn'][l][None],
        w_group=_hi_lo(jnp.pad(P['moe_w_group'][l], ((0, 0), (0, LANES - N_GROUPS)))),
        b_group=jnp.pad(P['moe_b_group'][l], (0, LANES - N_GROUPS))[None],
        w_router=_hi_lo(jnp.pad(P['moe_w_router'][l], ((0, 0), (0, LANES - N_GROUPS * EXPERTS_PER_GROUP)))),
        b_router=jnp.pad(P['moe_b_router'][l], (0, LANES - N_GROUPS * EXPERTS_PER_GROUP))[None],
        w_gate_up=P['moe_w_gate_up'][l].astype(BF16), w_down=P['moe_w_down'][l].astype(BF16),
    )


def _rope_table(pos):
    def cs(seg):
        half = seg // 2
        inv = ROPE_THETA ** (-jnp.arange(half, dtype=F32) / half)
        ang = pos.astype(F32)[:, None] * inv[None, :]
        c, s = jnp.cos(ang), jnp.sin(ang)
        reps = LANES // seg
        return jnp.tile(jnp.concatenate([c, c], axis=1), (1, reps)), jnp.tile(jnp.concatenate([-s, s], axis=1), (1, reps))
    c64, s64 = cs(64)
    c32, s32 = cs(32)
    return jnp.concatenate([c64, s64, c32, s32], axis=1)


def _rows(outs, B, L):
    sbq, sbkv, dfq, dfk, dfv, fxq, fxkv, misc, mq, lat, mk, mv = outs
    return (sbkv[:, :L].reshape(B, L, 2, 2, 64), dfk[:, :L].reshape(B, L, 2, 2, 64),
            dfv[:, :L].reshape(B, L, 2, 128), fxkv[:, :L].reshape(B, L, 2, 2, 64),
            misc[:, :L, 32:36], lat[:, :L], misc[:, :L, 0:32])


def _heads(a, n_outer, d):
    B, Lp, _ = a.shape
    return a.reshape(B, Lp, n_outer, d).transpose(0, 2, 1, 3)


def _unheads(o):
    B, H, G, Lp, d = o.shape
    return o.transpose(0, 3, 1, 2, 4).reshape(B * Lp, H * G * d)


def _prompt_layer(x, l, lw, tab, L):
    B, Lp, D = x.shape
    outs = _proj_call(x, tab, lw, TQ)
    sbq, sbkv, dfq, dfk, dfv, fxq, fxkv, misc, mq, lat, mk, mv = outs

    def qh(a, n_outer, g, d):
        return a.astype(BF16).reshape(B, Lp, n_outer, g, d).transpose(0, 2, 3, 1, 4)

    def kh(a, n_outer, d):
        return _heads(a.astype(BF16), n_outer, d)

    sb_kv = kh(sbkv, 4, 64)
    sb_o = _sb_call(qh(sbq, 2, 2, 64), sb_kv[:, 0:2], sb_kv[:, 2:4], 64 ** -0.5)
    df_o = _flash_call(qh(dfq, 4, 2, 64), kh(dfk, 4, 64), kh(dfv, 2, 128), 64 ** -0.5)
    logf = misc[:, :, 32:36].transpose(0, 2, 1).reshape(B * 4, Lp)
    cum = _cumsum_call(logf).reshape(B, 2, 2, Lp)
    fx_kv = kh(fxkv, 4, 64)
    fx_o = _flash_call(qh(fxq, 2, 2, 64), fx_kv[:, 0:2], fx_kv[:, 2:4], 64 ** -0.5, ck=cum[:, :, :, None, :])
    ml_q = jnp.concatenate([qh(mq[..., 0:256], 4, 1, 64), qh(mq[..., 256:384], 4, 1, 32)], axis=-1)
    ml_k = jnp.concatenate([kh(mk[..., 0:256], 4, 64), kh(mk[..., 256:384], 4, 32)], axis=-1)
    ml_o = _flash_call(ml_q, ml_k, kh(mv, 4, 64), 96 ** -0.5)

    lam_init = 0.8 - 0.6 * math.exp(-0.3 * l)
    xf = _merge_call(x.reshape(B * Lp, D), _unheads(sb_o), _unheads(df_o[:, 0:2]), _unheads(df_o[:, 2:4]),
                     _unheads(fx_o), _unheads(ml_o), lw, lam_init, TM_MERGE)
    xf = _moe_call(xf, lw, TM_MOE)
    return xf.reshape(B, Lp, D), _rows(outs, B, L)


def _block_diag_rows(q, n_heads, g, d):
    N = q.shape[0]
    q = q.reshape(N, n_heads, g, d)
    eye = jnp.eye(n_heads, dtype=q.dtype)
    out = jnp.einsum('nhgd,hk->nhgkd', q, eye).reshape(N, n_heads * g, n_heads * d)
    return jnp.pad(out, ((0, 0), (0, SUBLANES - n_heads * g), (0, 0)))


def _sample_layer(xs, l, lw, tab, page_table, caches, ind):
    N, D = xs.shape
    outs = _proj_call(xs[None], tab, lw, N)
    sbq, sbkv, dfq, dfk, dfv, fxq, fxkv, misc, mq, lat, mk, mv = [o[0] for o in outs]
    qsb = _block_diag_rows(sbq, 2, 2, 64)
    qdf = _block_diag_rows(dfq, 4, 2, 64)
    qfx = _block_diag_rows(fxq, 2, 2, 64)
    qmn = _block_diag_rows(mq[:, 0:256], 4, 1, 64)
    qmr = jnp.pad(mq[:, 256:384].reshape(N, 4, 32), ((0, 0), (0, 4), (0, 0)))
    new = (dfk[:, None], dfv[:, None], fxkv[:, None], lat[:, None], misc[:, None])
    u = jnp.asarray(np.tril(np.ones((128, 128), np.float32), -1), dtype=BF16)
    consts = (u, lw['wuk'], lw['wuv'], lw['gmk'], ind)
    osb, odf, ofx, oml = _decode_call(l, page_table, (qsb, qdf, qfx, qmn, qmr), new, consts, caches)
    sb_o = jnp.concatenate([osb[:, r, (r // 2) * 64:(r // 2 + 1) * 64] for r in range(4)], axis=-1)
    df0 = jnp.concatenate([odf[:, r, (r // 2) * 128:(r // 2 + 1) * 128] for r in range(4)], axis=-1)
    df1 = jnp.concatenate([odf[:, 4 + r, (r // 2) * 128:(r // 2 + 1) * 128] for r in range(4)], axis=-1)
    fx_o = jnp.concatenate([ofx[:, r, (r // 2) * 64:(r // 2 + 1) * 64] for r in range(4)], axis=-1)
    ml_o = jnp.concatenate([oml[:, r, r * 64:(r + 1) * 64] for r in range(4)], axis=-1)
    lam_init = 0.8 - 0.6 * math.exp(-0.3 * l)
    xf = _merge_call(xs, sb_o, df0, df1, fx_o, ml_o, lw, lam_init, N)
    xf = _moe_call(xf, lw, N)
    return xf, _rows([o[None] for o in (sbq, sbkv, dfq, dfk, dfv, fxq, fxkv, misc, mq, lat, mk, mv)], 1, N)


def kernel(x_prompt, x_sample, cache_sb_kv, cache_df_k, cache_df_v, cache_fx_kv, cache_fx_logf, cache_mla_latent, cache_mla_krope, page_table, meta_tokens, norm_mix, norm_ffn, w_in, w_gate, w_branch, w_out, df_q_norm, df_k_norm, df_lambda, df_out_norm, fx_q_norm, fx_k_norm, fx_forget_bias, mla_q_lora_norm, mla_kv_lora_norm, mla_w_uq, mla_w_ukv, mla_q_norm, mla_k_norm, moe_w_group, moe_b_group, moe_w_router, moe_b_router, moe_w_gate_up, moe_w_down):
    P = dict(norm_mix=norm_mix, norm_ffn=norm_ffn, w_in=w_in, w_gate=w_gate, w_branch=w_branch, w_out=w_out,
             df_q_norm=df_q_norm, df_k_norm=df_k_norm, df_lambda=df_lambda, df_out_norm=df_out_norm,
             fx_q_norm=fx_q_norm, fx_k_norm=fx_k_norm, fx_forget_bias=fx_forget_bias,
             mla_q_lora_norm=mla_q_lora_norm, mla_kv_lora_norm=mla_kv_lora_norm, mla_w_uq=mla_w_uq,
             mla_w_ukv=mla_w_ukv, mla_q_norm=mla_q_norm, mla_k_norm=mla_k_norm, moe_w_group=moe_w_group,
             moe_b_group=moe_b_group, moe_w_router=moe_w_router, moe_b_router=moe_b_router,
             moe_w_gate_up=moe_w_gate_up, moe_w_down=moe_w_down)
    depth = w_in.shape[0]
    B, S, D = x_prompt.shape
    L = S + N_META
    Lp = -(-L // TQ) * TQ
    lws = [_layer_weights(l, P) for l in range(depth)]

    meta = jnp.broadcast_to(meta_tokens[None].astype(x_prompt.dtype), (B, N_META, D))
    x = jnp.concatenate([meta, x_prompt, jnp.zeros((B, Lp - L, D), x_prompt.dtype)], axis=1)
    tab_p = _rope_table(jnp.arange(Lp, dtype=jnp.int32))
    rows_p = []
    for l in range(depth):
        x, rows = _prompt_layer(x, l, lws[l], tab_p, L)
        rows_p.append(rows)
    y_prompt = x[:, N_META:L]

    nb, dec_seq, _ = x_sample.shape
    assert dec_seq == 1
    n_pool, page = cache_sb_kv.shape[1], cache_sb_kv.shape[2]
    past_len = page_table.shape[1] * page
    tab_s = _rope_table(jnp.full((nb,), past_len, dtype=jnp.int32))
    assert page == LANES and n_pool % 2 == 0
    key_minor = (0, 1, 3, 4, 5, 2)
    caches = (cache_sb_kv.transpose(key_minor), cache_df_k.transpose(key_minor),
              cache_df_v.reshape(depth, n_pool, 2 * page, 128), cache_fx_kv.transpose(key_minor),
              cache_mla_latent, cache_mla_krope.transpose(0, 1, 3, 2),
              cache_fx_logf.transpose(0, 1, 3, 2).reshape(depth, n_pool // 2, SUBLANES, LANES))
    ind = _seg_ones(SUBLANES, 1, 256, 64)
    xs = x_sample.reshape(nb, D)
    rows_s = []
    for l in range(depth):
        xs, rows = _sample_layer(xs, l, lws[l], tab_s, page_table, caches, ind)
        rows_s.append(tuple(r.reshape((nb, 1) + r.shape[2:]) for r in rows))
    y_sample = xs.reshape(nb, 1, D)

    def stack(rl):
        return tuple(jnp.stack([r[i] for r in rl], axis=0) for i in range(len(rl[0])))

    sb_kv_p, df_k_p, df_v_p, fx_kv_p, fx_logf_p, mla_lat_p, mla_kr_p = stack(rows_p)
    sb_kv_s, df_k_s, df_v_s, fx_kv_s, fx_logf_s, mla_lat_s, mla_kr_s = stack(rows_s)
    return (y_prompt, y_sample, sb_kv_p, sb_kv_s, df_k_p, df_k_s, df_v_p, df_v_s, fx_kv_p, fx_kv_s,
            fx_logf_p, fx_logf_s, mla_lat_p, mla_lat_s, mla_kr_p, mla_kr_s)
```

```python
import functools
import math

import numpy as np
import jax
import jax.numpy as jnp
from jax import lax
from jax.experimental import pallas as pl
from jax.experimental.pallas import tpu as pltpu

F32 = jnp.float32
BF16 = jnp.bfloat16

N_META = 16
ROPE_THETA = 10000.0
EPS = 1e-6
NEG = -1e30
FX_FORGET_BIAS = 5.0
N_BRANCH = 4
N_GROUPS = 4
EXPERTS_PER_GROUP = 4

LANES = 128
SUBLANES = 8
VMEM_LIMIT = 56 * 1024 * 1024

TQ = 256
ROW_CHUNK = 512
KEY_BLOCKS_PER_ITER = 2
TM_MERGE = 256
TM_MOE = 512
PAGES_PER_STEP = 16
N_CACHES = 7
LF_CACHE = 6


def _cparams(sem):
    return pltpu.CompilerParams(dimension_semantics=sem, vmem_limit_bytes=VMEM_LIMIT)


def _dot(a, b):
    return jnp.dot(a, b, preferred_element_type=F32)


def _dot_nt(a, b):
    return lax.dot_general(a, b, (((1,), (1,)), ((), ())), preferred_element_type=F32)


def _split2(x):
    hi = x.astype(BF16)
    lo = (x - hi.astype(F32)).astype(BF16)
    return hi, lo


def _dot_hilo(x, w):
    hi, lo = _split2(x)
    return _dot(hi, w) + _dot(lo, w)


def _split3(x):
    a = x.astype(BF16)
    r = x - a.astype(F32)
    b = r.astype(BF16)
    c = (r - b.astype(F32)).astype(BF16)
    return a, b, c


def _dot_3way(x, w):
    a, b, c = _split3(x)
    return _dot(a, w) + _dot(b, w) + _dot(c, w)


def _log_sigmoid(z):
    return jnp.minimum(z, 0.0) - jnp.log1p(jnp.exp(-jnp.abs(z)))


def _lane_iota(shape):
    return lax.broadcasted_iota(jnp.int32, shape, len(shape) - 1)


def _swap_halves(y, seg):
    half = seg // 2
    lane = _lane_iota(y.shape)
    fwd = pltpu.roll(y, LANES - half, 1)
    bwd = pltpu.roll(y, half, 1)
    return jnp.where((lane % seg) < half, fwd, bwd)


def _seg_ones(n_rows, row_seg, n_cols, col_seg):
    r = np.arange(n_rows)[:, None] // row_seg
    c = np.arange(n_cols)[None, :] // col_seg
    return jnp.asarray((r == c).astype(np.float32), dtype=BF16)


def _mla_head_matrix():
    head = np.concatenate([np.arange(256) // 64, np.arange(128) // 32])
    return jnp.asarray((head[:, None] == head[None, :]).astype(np.float32), dtype=BF16)


def _proj_kernel(x_ref, tab_ref, gmix_ref, w_ref, g64_ref, fb_ref, gcq_ref, wuq_ref, gmq_ref,
                 glat_ref, wukv_ref, gmk_ref, s64_ref, s256_ref, m96_ref, snr_ref,
                 sbq_ref, sbkv_ref, dfq_ref, dfk_ref, dfv_ref, fxq_ref, fxkv_ref, misc_ref,
                 mq_ref, lat_ref, mk_ref, mv_ref):
    x = x_ref[...]
    h = x * lax.rsqrt(jnp.mean(x * x, axis=-1, keepdims=True) + EPS) * gmix_ref[...]
    hb = h.astype(BF16)
    tab = tab_ref[...]
    cos64, sin64 = tab[:, 0:128], tab[:, 128:256]
    cos32, sin32 = tab[:, 256:384], tab[:, 384:512]
    s64 = s64_ref[...]
    g64 = g64_ref[...]

    def proj(a, b):
        return _dot(hb, w_ref[:, a:b])

    def segnorm64(y, g):
        ms = _dot_hilo(y * y, s64) * (1.0 / 64.0)
        return y * lax.rsqrt(ms + EPS) * g

    def rope64(y):
        return y * cos64 + _swap_halves(y, 64) * sin64

    sbq_ref[...] = proj(0, 256)
    sbkv_ref[...] = proj(256, 512)
    for c in range(4):
        y = proj(512 + 128 * c, 640 + 128 * c)
        dfq_ref[:, 128 * c:128 * (c + 1)] = rope64(segnorm64(y, g64[0:1]))
    for c in range(2):
        y = proj(1024 + 128 * c, 1152 + 128 * c)
        dfk_ref[:, 128 * c:128 * (c + 1)] = rope64(segnorm64(y, g64[1:2]))
    dfv_ref[...] = proj(1280, 1536)
    for c in range(2):
        y = proj(1536 + 128 * c, 1664 + 128 * c)
        fxq_ref[:, 128 * c:128 * (c + 1)] = segnorm64(y, g64[2:3])
    fxkv_ref[:, 0:128] = segnorm64(proj(1792, 1920), g64[3:4])
    fxkv_ref[:, 128:256] = proj(1920, 2048)

    ym = proj(2432, 2560)
    lane = _lane_iota(ym.shape)
    kr = ym * cos32 + _swap_halves(ym, 32) * sin32
    logf = _log_sigmoid(ym + fb_ref[...])
    misc = jnp.where(lane < 32, kr, jnp.where(lane < 36, logf, 0.0))
    misc_ref[...] = misc

    cq = proj(2048, 2304)
    cq = cq * lax.rsqrt(jnp.mean(cq * cq, axis=-1, keepdims=True) + EPS) * gcq_ref[...]
    qf = _dot(cq.astype(BF16), wuq_ref[...])
    qr = qf[:, 256:384]
    qr = qr * cos32 + _swap_halves(qr, 32) * sin32
    qf = jnp.concatenate([qf[:, 0:256], qr], axis=1)
    ms = _dot_hilo(qf * qf, m96_ref[...]) * (1.0 / 96.0)
    mq_ref[...] = qf * lax.rsqrt(ms + EPS) * gmq_ref[...]

    ckv = proj(2304, 2432)
    lat = ckv * lax.rsqrt(jnp.mean(ckv * ckv, axis=-1, keepdims=True) + EPS) * glat_ref[...]
    lat_ref[...] = lat

    kv = _dot(lat.astype(BF16), wukv_ref[...])
    kn = kv[:, 0:256]
    mv_ref[...] = kv[:, 256:512]
    krm = jnp.where(lane < 32, kr, 0.0)
    n2 = jnp.sum(krm * krm, axis=-1, keepdims=True)
    kn2h, kn2l = _split2(kn * kn)
    s256 = s256_ref[...]
    n1 = _dot(kn2h, s256) + _dot(kn2l, s256)
    n1r = _dot(kn2h, snr_ref[...]) + _dot(kn2l, snr_ref[...])
    gmk = gmk_ref[...]
    mk_ref[:, 0:256] = kn * lax.rsqrt((n1 + n2) * (1.0 / 96.0) + EPS) * gmk[:, 0:256]
    krt = krm + pltpu.roll(krm, 32, 1) + pltpu.roll(krm, 64, 1) + pltpu.roll(krm, 96, 1)
    mk_ref[:, 256:384] = krt * lax.rsqrt((n1r + n2) * (1.0 / 96.0) + EPS) * gmk[:, 256:384]


def _proj_call(x, tab, lw, tm):
    B, Lp, D = x.shape
    nl = Lp // tm
    widths = (256, 256, 512, 256, 256, 256, 256, 128, 384, 128, 384, 256)

    def tok(w):
        return pl.BlockSpec((None, tm, w), lambda b, i: (b, i, 0))

    def full(a):
        return pl.BlockSpec(a.shape, lambda b, i: (0,) * a.ndim)

    consts = (lw['gmix'], lw['w_in'], lw['g64'], lw['fb'], lw['gcq'], lw['wuq'], lw['gmq'], lw['glat'],
              lw['wukv'], lw['gmk'], lw['s64'], lw['s256'], lw['m96'], lw['snr'])
    return pl.pallas_call(
        _proj_kernel,
        grid=(B, nl),
        in_specs=[tok(D), pl.BlockSpec((tm, 512), lambda b, i: (i, 0))] + [full(a) for a in consts],
        out_specs=[tok(w) for w in widths],
        out_shape=[jax.ShapeDtypeStruct((B, Lp, w), F32) for w in widths],
        compiler_params=_cparams(("parallel", "parallel")),
        name="proj",
    )(x, tab, *consts)


def _lane_tile(x, width):
    return x if width == LANES else jnp.concatenate([x] * (width // LANES), axis=1)


def _q_rows(q_ref, r0, ch):
    G, tq, d = q_ref.shape
    if ch <= tq:
        g, q0 = divmod(r0, tq)
        return q_ref[g, q0:q0 + ch, :]
    return q_ref[r0 // tq:(r0 + ch) // tq].reshape(ch, d)


def _rows_of_heads(per_head, r0, ch, tq):
    if ch <= tq:
        return per_head[r0 // tq]
    n = per_head[0].shape[-1]
    return jnp.concatenate([jnp.broadcast_to(per_head[g], (tq, n))
                            for g in range(r0 // tq, (r0 + ch) // tq)], axis=0)


def _flash_kernel(*refs, G, tq, scale, use_bias):
    if use_bias:
        q_ref, k_ref, v_ref, ck_ref, o_ref, m_sc, l_sc, acc_sc = refs
    else:
        q_ref, k_ref, v_ref, o_ref, m_sc, l_sc, acc_sc = refs
    i = pl.program_id(2)
    dv = v_ref.shape[-1]
    rows = G * tq
    ch = min(ROW_CHUNK, rows)
    m_sc[...] = jnp.full((rows, LANES), NEG, F32)
    l_sc[...] = jnp.zeros((rows, LANES), F32)
    acc_sc[...] = jnp.zeros((rows, dv), F32)
    if use_bias:
        c0 = ck_ref[:, :, pl.ds(pl.multiple_of(i * tq, tq), LANES)][:, :, 0:1]

    def step(j, width, masked):
        start = pl.multiple_of(j * tq, tq)
        k = k_ref[pl.ds(start, width), :]
        v = v_ref[pl.ds(start, width), :]
        if use_bias:
            bias = c0 - ck_ref[:, :, pl.ds(start, width)]
        for r0 in range(0, rows, ch):
            s = _dot_nt(_q_rows(q_ref, r0, ch), k) * scale
            if use_bias:
                s = s + _rows_of_heads([bias[g] for g in range(G)], r0, ch, tq)
            if masked:
                qpos = (r0 + lax.broadcasted_iota(jnp.int32, (ch, width), 0)) % tq
                kpos = lax.broadcasted_iota(jnp.int32, (ch, width), 1)
                s = jnp.where(kpos <= qpos, s, NEG)
            m_old = m_sc[r0:r0 + ch, :]
            m_new = jnp.maximum(m_old, jnp.max(s, axis=-1, keepdims=True))
            p = jnp.exp(s - _lane_tile(m_new, width))
            alpha = jnp.exp(m_old - m_new)
            l_sc[r0:r0 + ch, :] = alpha * l_sc[r0:r0 + ch, :] + jnp.sum(p, axis=-1, keepdims=True)
            acc_sc[r0:r0 + ch, :] = alpha[:, 0:dv] * acc_sc[r0:r0 + ch, :] + _dot(p.astype(BF16), v)
            m_sc[r0:r0 + ch, :] = m_new

    def body(jj, c):
        step(KEY_BLOCKS_PER_ITER * jj, KEY_BLOCKS_PER_ITER * tq, False)
        return c

    n_wide = i // KEY_BLOCKS_PER_ITER
    lax.fori_loop(0, n_wide, body, 0)

    def tail(j, c):
        step(j, tq, False)
        return c

    lax.fori_loop(n_wide * KEY_BLOCKS_PER_ITER, i, tail, 0)
    step(i, tq, True)
    o = acc_sc[...] / l_sc[:, 0:dv]
    o_ref[...] = o.reshape(G, tq, dv)


def _flash_call(q, k, v, scale, ck=None):
    B, H, G, Lp, dq = q.shape
    Hv, dv = v.shape[1], v.shape[-1]
    nq = Lp // TQ
    use_bias = ck is not None
    in_specs = [
        pl.BlockSpec((None, None, G, TQ, dq), lambda b, h, i: (b, h, 0, i, 0)),
        pl.BlockSpec((None, None, Lp, dq), lambda b, h, i: (b, h, 0, 0)),
        pl.BlockSpec((None, None, Lp, dv), lambda b, h, i: (b, h % Hv, 0, 0)),
    ]
    args = [q, k, v]
    if use_bias:
        in_specs.append(pl.BlockSpec((None, None, G, 1, Lp), lambda b, h, i: (b, h, 0, 0, 0)))
        args.append(ck)
    return pl.pallas_call(
        functools.partial(_flash_kernel, G=G, tq=TQ, scale=scale, use_bias=use_bias),
        grid=(B, H, nq),
        in_specs=in_specs,
        out_specs=pl.BlockSpec((None, None, G, TQ, dv), lambda b, h, i: (b, h, 0, i, 0)),
        out_shape=jax.ShapeDtypeStruct((B, H, G, Lp, dv), F32),
        scratch_shapes=[pltpu.VMEM((G * TQ, LANES), F32), pltpu.VMEM((G * TQ, LANES), F32),
                        pltpu.VMEM((G * TQ, dv), F32)],
        compiler_params=_cparams(("parallel", "parallel", "arbitrary")),
        name="flash_bias" if use_bias else "flash",
    )(*args)


def _sb_kernel(q_ref, k_ref, v_ref, u_ref, o_ref, r_sc, acc_sc, *, G, tq, scale):
    i = pl.program_id(2)
    dq = q_ref.shape[-1]
    rows = G * tq
    ch = min(ROW_CHUNK, rows)
    r_sc[...] = jnp.zeros((rows, LANES), F32)
    acc_sc[...] = jnp.zeros((rows, dq), F32)

    def step(j, nblk, masked):
        start = pl.multiple_of(j * tq, tq)
        k = k_ref[pl.ds(start, nblk * tq), :]
        v = v_ref[pl.ds(start, nblk * tq), :]
        u = u_ref[...]
        for r0 in range(0, rows, ch):
            z = _dot_nt(_q_rows(q_ref, r0, ch), k) * scale
            ls = _log_sigmoid(z)
            lk = ls - z
            if masked:
                qpos = (r0 + lax.broadcasted_iota(jnp.int32, (ch, tq), 0)) % tq
                kpos = lax.broadcasted_iota(jnp.int32, (ch, tq), 1)
                vis = kpos < qpos
                lk = jnp.where(vis, lk, 0.0)
            r = _lane_tile(r_sc[r0:r0 + ch, :], tq)
            later = [None] * nblk
            for bk in reversed(range(nblk)):
                lkb = lk[:, bk * tq:(bk + 1) * tq]
                later[bk] = _dot_hilo(lkb, u) + r
                r = r + jnp.sum(lkb, axis=-1, keepdims=True)
            w = jnp.exp(ls + (later[0] if nblk == 1 else jnp.concatenate(later, axis=1)))
            if masked:
                w = jnp.where(vis, w, 0.0)
            acc_sc[r0:r0 + ch, :] += _dot(w.astype(BF16), v)
            r_sc[r0:r0 + ch, :] = r[:, 0:LANES]

    step(i, 1, True)
    kb = KEY_BLOCKS_PER_ITER
    n_wide = i // kb

    def body(t, c):
        step(i - kb * (t + 1), kb, False)
        return c

    lax.fori_loop(0, n_wide, body, 0)

    def tail(t, c):
        step(i - kb * n_wide - 1 - t, 1, False)
        return c

    lax.fori_loop(0, i - kb * n_wide, tail, 0)
    o_ref[...] = acc_sc[...].reshape(G, tq, dq)


def _sb_call(q, k, v, scale):
    B, H, G, Lp, dq = q.shape
    nq = Lp // TQ
    u = jnp.asarray(np.tril(np.ones((TQ, TQ), np.float32), -1), dtype=BF16)
    return pl.pallas_call(
        functools.partial(_sb_kernel, G=G, tq=TQ, scale=scale),
        grid=(B, H, nq),
        in_specs=[
            pl.BlockSpec((None, None, G, TQ, dq), lambda b, h, i: (b, h, 0, i, 0)),
            pl.BlockSpec((None, None, Lp, dq), lambda b, h, i: (b, h, 0, 0)),
            pl.BlockSpec((None, None, Lp, dq), lambda b, h, i: (b, h, 0, 0)),
            pl.BlockSpec((TQ, TQ), lambda b, h, i: (0, 0)),
        ],
        out_specs=pl.BlockSpec((None, None, G, TQ, dq), lambda b, h, i: (b, h, 0, i, 0)),
        out_shape=jax.ShapeDtypeStruct((B, H, G, Lp, dq), F32),
        scratch_shapes=[pltpu.VMEM((G * TQ, LANES), F32), pltpu.VMEM((G * TQ, dq), F32)],
        compiler_params=_cparams(("parallel", "parallel", "arbitrary")),
        name="sb",
    )(q, k, v, u)


def _cumsum_kernel(x_ref, u_ref, o_ref, *, blk):
    n = x_ref.shape[-1] // blk
    u = u_ref[...]
    carry = jnp.zeros((x_ref.shape[0], 1), F32)
    for j in range(n):
        x = x_ref[:, j * blk:(j + 1) * blk]
        c = _dot_3way(x, u) + carry
        o_ref[:, j * blk:(j + 1) * blk] = c
        carry = c[:, blk - 1:blk]


def _cumsum_call(x):
    R, Lp = x.shape
    blk = 256
    u = jnp.asarray(np.triu(np.ones((blk, blk), np.float32)), dtype=BF16)
    return pl.pallas_call(
        functools.partial(_cumsum_kernel, blk=blk),
        out_shape=jax.ShapeDtypeStruct((R, Lp), F32),
        name="logf_cumsum",
    )(x, u)


def _merge_kernel(x_ref, sb_ref, df0_ref, df1_ref, fx_ref, ml_ref, gmix_ref, lam_ref, gdf_ref,
                  wg_ref, wb_ref, wo_ref, o_ref, *, lam_init):
    x = x_ref[...]
    D = x.shape[-1]
    h = (x * lax.rsqrt(jnp.mean(x * x, axis=-1, keepdims=True) + EPS) * gmix_ref[...]).astype(BF16)
    lp = lam_ref[...]
    lam = (jnp.exp(jnp.sum(lp[0:1] * lp[1:2], axis=-1, keepdims=True))
           - jnp.exp(jnp.sum(lp[2:3] * lp[3:4], axis=-1, keepdims=True)) + lam_init)
    gdf = gdf_ref[...]
    dfs = []
    for c in range(4):
        a = df0_ref[:, 128 * c:128 * (c + 1)] - lam * df1_ref[:, 128 * c:128 * (c + 1)]
        a = a * lax.rsqrt(jnp.mean(a * a, axis=-1, keepdims=True) + EPS) * gdf
        dfs.append(a * (1.0 - lam_init))
    df = jnp.concatenate(dfs, axis=1)
    merged = jnp.zeros(x.shape, F32)
    off = 0
    for b, o in enumerate((sb_ref[...], df, fx_ref[...], ml_ref[...])):
        width = o.shape[-1]
        gate = jax.nn.sigmoid(_dot(h, wg_ref[:, b * D:(b + 1) * D]))
        merged = merged + gate * _dot(o.astype(BF16), wb_ref[off:off + width, :])
        off += width
    o_ref[...] = x + _dot(merged.astype(BF16), wo_ref[...])


def _merge_call(x, sb, df0, df1, fx, ml, lw, lam_init, tm):
    N, D = x.shape
    assert N % tm == 0

    def tok(w):
        return pl.BlockSpec((tm, w), lambda i: (i, 0))

    def full(a):
        return pl.BlockSpec(a.shape, lambda i: (0,) * a.ndim)

    consts = (lw['gmix'], lw['lam'], lw['gdf'], lw['w_gate'], lw['w_branch'], lw['w_out'])
    return pl.pallas_call(
        functools.partial(_merge_kernel, lam_init=lam_init),
        grid=(N // tm,),
        in_specs=[tok(D), tok(256), tok(512), tok(512), tok(256), tok(256)] + [full(a) for a in consts],
        out_specs=tok(D),
        out_shape=jax.ShapeDtypeStruct((N, D), F32),
        compiler_params=_cparams(("parallel",)),
        name="merge",
    )(x, sb, df0, df1, fx, ml, *consts)


def _moe_kernel(x_ref, gffn_ref, wgr_ref, bgr_ref, wrt_ref, brt_ref, wgu_ref, wdn_ref, o_ref,
                h_sc, comb_sc, acc_sc):
    gi = pl.program_id(1)
    tm = x_ref.shape[0]
    hid = wdn_ref.shape[1]

    @pl.when(gi == 0)
    def _():
        x = x_ref[...]
        h, h_lo = _split2(x * lax.rsqrt(jnp.mean(x * x, axis=-1, keepdims=True) + EPS) * gffn_ref[...])
        h_sc[...] = h

        def logits(w_ref):
            return _dot(h, w_ref[0]) + (_dot(h_lo, w_ref[0]) + _dot(h, w_ref[1]))

        lane = _lane_iota((tm, LANES))
        big = jnp.int32(LANES)
        gl = jnp.where(lane < N_GROUPS, logits(wgr_ref) + bgr_ref[...], NEG)
        gmax = jnp.max(gl, axis=-1, keepdims=True)
        gidx = jnp.min(jnp.where(gl == gmax, lane, big), axis=-1, keepdims=True)
        g_w = 1.0 / jnp.sum(jnp.exp(gl - gmax), axis=-1, keepdims=True)
        sel = (lane // EXPERTS_PER_GROUP) == gidx
        el = jnp.where(sel, logits(wrt_ref) + brt_ref[...], NEG)
        emax = jnp.max(el, axis=-1, keepdims=True)
        pe = jnp.exp(el - emax)
        p = pe / jnp.sum(pe, axis=-1, keepdims=True)
        p = jnp.where(sel, p, -1.0)
        top1 = jnp.max(p, axis=-1, keepdims=True)
        i1 = jnp.min(jnp.where(p == top1, lane, big), axis=-1, keepdims=True)
        p2 = jnp.where(lane == i1, -1.0, p)
        top2 = jnp.max(p2, axis=-1, keepdims=True)
        i2 = jnp.min(jnp.where(p2 == top2, lane, big), axis=-1, keepdims=True)
        den = top1 + top2
        comb_sc[...] = (jnp.where(lane == i1, top1 / den * g_w, 0.0)
                        + jnp.where(lane == i2, top2 / den * g_w, 0.0))
        acc_sc[...] = jnp.zeros(acc_sc.shape, F32)

    h = h_sc[...]
    comb_hi, comb_lo = _split2(comb_sc[...])
    row = lax.broadcasted_iota(jnp.int32, (LANES, hid), 0)
    for e in range(EXPERTS_PER_GROUP):
        sel = jnp.where(row == gi * EXPERTS_PER_GROUP + e, 1.0, 0.0).astype(BF16)
        cb = _dot(comb_hi, sel) + _dot(comb_lo, sel)
        gu = _dot(h, wgu_ref[e])
        g = gu[:, :hid]
        act = g * jax.nn.sigmoid(g) * gu[:, hid:] * cb
        acc_sc[...] += _dot(act.astype(BF16), wdn_ref[e])

    @pl.when(gi == pl.num_programs(1) - 1)
    def _():
        o_ref[...] = x_ref[...] + acc_sc[...]


def _moe_call(x, lw, tm):
    N, D = x.shape
    assert N % tm == 0
    hid = lw['w_down'].shape[1]
    epg = EXPERTS_PER_GROUP

    def full(a):
        return pl.BlockSpec(a.shape, lambda i, g: (0,) * a.ndim)

    return pl.pallas_call(
        _moe_kernel,
        grid=(N // tm, N_GROUPS),
        in_specs=[pl.BlockSpec((tm, D), lambda i, g: (i, 0)),
                  full(lw['gffn']), full(lw['w_group']), full(lw['b_group']),
                  full(lw['w_router']), full(lw['b_router']),
                  pl.BlockSpec((epg, D, 2 * hid), lambda i, g: (g, 0, 0)),
                  pl.BlockSpec((epg, hid, D), lambda i, g: (g, 0, 0))],
        out_specs=pl.BlockSpec((tm, D), lambda i, g: (i, 0)),
        out_shape=jax.ShapeDtypeStruct((N, D), F32),
        scratch_shapes=[pltpu.VMEM((tm, D), BF16), pltpu.VMEM((tm, LANES), F32), pltpu.VMEM((tm, D), F32)],
        compiler_params=_cparams(("parallel", "arbitrary")),
        name="moe",
    )(x, lw['gffn'], lw['w_group'], lw['b_group'], lw['w_router'], lw['b_router'],
      lw['w_gate_up'], lw['w_down'])


def _decode_kernel(pt_ref, qsb_ref, qdf_ref, qfx_ref, qmn_ref, qmr_ref,
                   ndfk_ref, ndfv_ref, nfxkv_ref, nlat_ref, nmisc_ref,
                   u_ref, wuk_ref, wuv_ref, gmk_ref, ind_ref, *rest, npp, n_pages, layer):
    caches = rest[:N_CACHES]
    osb_ref, odf_ref, ofx_ref, oml_ref = rest[N_CACHES:N_CACHES + 4]
    (sb_acc, sb_car, df_m, df_l, df_acc, fx_m, fx_l, fx_acc, fx_car,
     ml_m, ml_l, ml_acc) = rest[N_CACHES + 4:N_CACHES + 16]
    bufs = rest[N_CACHES + 16:2 * N_CACHES + 16]
    sems = rest[2 * N_CACHES + 16]
    b = pl.program_id(0)
    c = pl.program_id(1)
    nsteps = pl.num_programs(1)
    R = SUBLANES
    t = b * nsteps + c
    slot = t % 2

    def page_copies(step, to_slot):
        sb_, sc_ = step // nsteps, step % nsteps
        out = []
        for p in range(npp):
            idx = pt_ref[sb_ * n_pages + (nsteps - 1 - sc_) * npp + p]
            for k in range(N_CACHES):
                src = caches[k].at[layer, idx // 2] if k == LF_CACHE else caches[k].at[layer, idx]
                out.append(pltpu.make_async_copy(src, bufs[k].at[to_slot, p], sems.at[to_slot, k]))
        return out

    @pl.when(t == 0)
    def _():
        for cp in page_copies(t, slot):
            cp.start()

    @pl.when(t + 1 < pl.num_programs(0) * nsteps)
    def _():
        for cp in page_copies(t + 1, 1 - slot):
            cp.start()

    for cp in page_copies(t, slot):
        cp.wait()
    pages = [bufs[k].at[slot, p] for p in range(npp) for k in range(N_CACHES)]

    @pl.when(c == 0)
    def _():
        sb_acc[...] = jnp.zeros(sb_acc.shape, F32)
        sb_car[...] = jnp.zeros(sb_car.shape, F32)
        for m_sc, l_sc, a_sc in ((df_m, df_l, df_acc), (fx_m, fx_l, fx_acc), (ml_m, ml_l, ml_acc)):
            m_sc[...] = jnp.full(m_sc.shape, NEG, F32)
            l_sc[...] = jnp.zeros(l_sc.shape, F32)
            a_sc[...] = jnp.zeros(a_sc.shape, F32)
        nm = nmisc_ref[...]
        rowi = lax.broadcasted_iota(jnp.int32, (R, LANES), 0)
        lanei = lax.broadcasted_iota(jnp.int32, (R, LANES), 1)
        pick = jnp.where(lanei == 32 + (rowi % 4), jnp.broadcast_to(nm, (R, LANES)), 0.0)
        fx_car[...] = jnp.sum(pick, axis=-1, keepdims=True)

    qsb = qsb_ref[...].astype(BF16)
    qdf = qdf_ref[...].astype(BF16)
    qfx = qfx_ref[...].astype(BF16)
    gmk = gmk_ref[...]
    qmn = (qmn_ref[...] * gmk[:, 0:256]).astype(BF16)
    qmr = (qmr_ref[...] * gmk[:, 256:288]).astype(BF16)
    ind = ind_ref[...]
    ones_r = jnp.ones((R, 32), BF16)
    wuk = wuk_ref[...]

    def rowsum(x):
        return jnp.sum(x, axis=-1, keepdims=True)

    sb_ls, sb_lk, df_s, fx_s, ml_s, lf_rows, latb = [], [], [], [], [], [], []
    for p in range(npp):
        sbT, dfkT, _, fxT, lat_ref, krT_ref, lf_ref = pages[7 * p:7 * p + 7]
        z = _dot(qsb, sbT[0].reshape(128, 128).astype(BF16)) * 0.125
        ls = _log_sigmoid(z)
        sb_ls.append(ls)
        sb_lk.append(ls - z)
        df_s.append(_dot(qdf, dfkT[...].reshape(256, 128).astype(BF16)) * 0.125)
        fx_s.append(_dot(qfx, fxT[0].reshape(128, 128).astype(BF16)) * 0.125)
        lb = lat_ref[...].astype(BF16)
        latb.append(lb)
        kn = _dot(lb, wuk)
        kr = krT_ref[...]
        a = _dot_nt(qmn, kn.astype(BF16)) + _dot(qmr, kr.astype(BF16))
        n = _dot_nt(ind, (kn * kn).astype(BF16)) + _dot(ones_r, (kr * kr).astype(BF16))
        ml_s.append(a * lax.rsqrt(n * (1.0 / 96.0) + EPS) * (96.0 ** -0.5))
        j = (nsteps - 1 - c) * npp + p
        odd = (pt_ref[b * n_pages + j] % 2) == 1
        blk = lf_ref[...]
        lf_rows.append(jnp.where(odd, pltpu.roll(blk, 4, 0), blk))

    pieces = []
    for p in range(npp):
        pieces += list(_split2(sb_lk[p])) + list(_split3(lf_rows[p]))
    suf = _dot(jnp.concatenate(pieces, axis=0), u_ref[...])
    sb_suf = [suf[5 * R * p:5 * R * p + R] + suf[5 * R * p + R:5 * R * p + 2 * R] for p in range(npp)]
    fx_suf = [suf[5 * R * p + 2 * R:5 * R * p + 3 * R] + suf[5 * R * p + 3 * R:5 * R * p + 4 * R]
              + suf[5 * R * p + 4 * R:5 * R * p + 5 * R] for p in range(npp)]

    car = sb_car[...]
    acc = sb_acc[...]
    for p in reversed(range(npp)):
        w = jnp.exp(sb_ls[p] + sb_suf[p] + car)
        acc = acc + _dot_nt(w.astype(BF16), pages[7 * p][1].reshape(128, 128).astype(BF16))
        car = car + rowsum(sb_lk[p])
    sb_acc[...] = acc
    sb_car[...] = car

    car = fx_car[...]
    for p in reversed(range(npp)):
        fx_s[p] = fx_s[p] + fx_suf[p] + car
        car = car + rowsum(lf_rows[p])
    fx_car[...] = car

    def softmax_step(m_sc, l_sc, a_sc, s_list, pv):
        m_old = m_sc[...]
        smax = s_list[0]
        for s in s_list[1:]:
            smax = jnp.maximum(smax, s)
        m_new = jnp.maximum(m_old, jnp.max(smax, axis=-1, keepdims=True))
        alpha = jnp.exp(m_old - m_new)
        ps = [jnp.exp(s - m_new) for s in s_list]
        psum = ps[0]
        for x in ps[1:]:
            psum = psum + x
        l_sc[...] = alpha * l_sc[...] + rowsum(psum)
        acc = alpha * a_sc[...]
        for p in range(npp):
            acc = acc + pv(p, ps[p].astype(BF16))
        a_sc[...] = acc
        m_sc[...] = m_new

    def df_pv(p, pb):
        dfv = pages[7 * p + 2]
        v0 = dfv[pl.ds(0, 128, stride=2), :].astype(BF16)
        v1 = dfv[pl.ds(1, 128, stride=2), :].astype(BF16)
        return jnp.concatenate([_dot(pb, v0), _dot(pb, v1)], axis=1)

    softmax_step(df_m, df_l, df_acc, df_s, df_pv)
    softmax_step(fx_m, fx_l, fx_acc, fx_s,
                 lambda p, pb: _dot_nt(pb, pages[7 * p + 3][1].reshape(128, 128).astype(BF16)))
    softmax_step(ml_m, ml_l, ml_acc, ml_s, lambda p, pb: _dot(pb, latb[p]))

    @pl.when(c == nsteps - 1)
    def _():
        def finish(m_sc, l_sc, a_sc, s_new, v_new):
            m_old = m_sc[...]
            m_new = jnp.maximum(m_old, s_new)
            p = jnp.exp(s_new - m_new)
            alpha = jnp.exp(m_old - m_new)
            return (alpha * a_sc[...] + p * v_new) / (alpha * l_sc[...] + p)

        osb_ref[...] = sb_acc[...]
        s_new = jnp.sum(qdf_ref[...] * ndfk_ref[...], axis=-1, keepdims=True) * 0.125
        odf_ref[...] = finish(df_m, df_l, df_acc, s_new, ndfv_ref[...])
        nfxkv = nfxkv_ref[...]
        s_new = jnp.sum(qfx_ref[...] * nfxkv[:, 0:128], axis=-1, keepdims=True) * 0.125
        ofx_ref[...] = finish(fx_m, fx_l, fx_acc, s_new, nfxkv[:, 128:256])
        nlat = nlat_ref[...]
        nlat8 = jnp.broadcast_to(nlat, (R, LANES))
        kn = _dot(nlat8.astype(BF16), wuk)
        krn = nmisc_ref[...][:, 0:32]
        indf = ind.astype(F32)
        a = (jnp.sum(qmn_ref[...] * gmk[:, 0:256] * kn, axis=-1, keepdims=True)
             + jnp.sum(qmr_ref[...] * gmk[:, 256:288] * krn, axis=-1, keepdims=True))
        n = (jnp.sum(indf * kn * kn, axis=-1, keepdims=True)
             + jnp.sum(krn * krn, axis=-1, keepdims=True))
        s_new = a * lax.rsqrt(n * (1.0 / 96.0) + EPS) * (96.0 ** -0.5)
        pl_ = finish(ml_m, ml_l, ml_acc, s_new, nlat8)
        oml_ref[...] = _dot(pl_.astype(BF16), wuv_ref[...])


def _decode_call(l, page_table, q, new, consts, caches):
    nb, n_pages = page_table.shape
    npp = PAGES_PER_STEP
    assert n_pages % npp == 0
    nsteps = n_pages // npp
    pt_flat = page_table.reshape(-1)

    def seq(a):
        return pl.BlockSpec((None,) + a.shape[1:], lambda b, c, pt: (b,) + (0,) * (a.ndim - 1))

    def full(a):
        return pl.BlockSpec(a.shape, lambda b, c, pt: (0,) * a.ndim)

    assert len(caches) == N_CACHES
    in_specs = ([seq(a) for a in q] + [seq(a) for a in new] + [full(a) for a in consts]
                + [pl.BlockSpec(memory_space=pl.ANY)] * N_CACHES)
    args = list(q) + list(new) + list(consts) + list(caches)
    out_widths = (128, 256, 128, 256)
    grid_spec = pltpu.PrefetchScalarGridSpec(
        num_scalar_prefetch=1,
        grid=(nb, nsteps),
        in_specs=in_specs,
        out_specs=[pl.BlockSpec((None, SUBLANES, w), lambda b, c, pt: (b, 0, 0)) for w in out_widths],
        scratch_shapes=[pltpu.VMEM((SUBLANES, 128), F32), pltpu.VMEM((SUBLANES, 1), F32),
                        pltpu.VMEM((SUBLANES, 1), F32), pltpu.VMEM((SUBLANES, 1), F32),
                        pltpu.VMEM((SUBLANES, 256), F32),
                        pltpu.VMEM((SUBLANES, 1), F32), pltpu.VMEM((SUBLANES, 1), F32),
                        pltpu.VMEM((SUBLANES, 128), F32), pltpu.VMEM((SUBLANES, 1), F32),
                        pltpu.VMEM((SUBLANES, 1), F32), pltpu.VMEM((SUBLANES, 1), F32),
                        pltpu.VMEM((SUBLANES, 128), F32)]
        + [pltpu.VMEM((2, npp) + a.shape[2:], a.dtype) for a in caches]
        + [pltpu.SemaphoreType.DMA((2, N_CACHES))],
    )
    return pl.pallas_call(
        functools.partial(_decode_kernel, npp=npp, n_pages=n_pages, layer=l),
        grid_spec=grid_spec,
        out_shape=[jax.ShapeDtypeStruct((nb, SUBLANES, w), F32) for w in out_widths],
        compiler_params=_cparams(("arbitrary", "arbitrary")),
        name="decode",
    )(pt_flat, *args)


def _hi_lo(w):
    return jnp.stack(_split2(w))


def _layer_weights(l, P):
    D = P['w_in'].shape[1]
    w_in = P['w_in'][l]
    w_in_p = jnp.concatenate([w_in[:, :2048], w_in[:, 2052:2436], w_in[:, 2436:2468], w_in[:, 2048:2052],
                              jnp.zeros((D, 92), F32)], axis=1).astype(BF16)
    t2 = lambda g: jnp.tile(g, 2)
    g64 = jnp.stack([t2(P['df_q_norm'][l]), t2(P['df_k_norm'][l]), t2(P['fx_q_norm'][l]), t2(P['fx_k_norm'][l])])
    g64 = jnp.concatenate([g64, jnp.zeros((4, 128), F32)], axis=0)
    fb = jnp.zeros((1, 128), F32).at[0, 32:36].set(P['fx_forget_bias'][l])
    wuq = P['mla_w_uq'][l]
    nh = wuq.shape[1]
    wuq_p = jnp.concatenate([wuq[:, :, :64].reshape(-1, nh * 64), wuq[:, :, 64:].reshape(-1, nh * 32)], axis=1)
    gq = P['mla_q_norm'][l]
    gmq = jnp.concatenate([jnp.tile(gq[:64], nh), jnp.tile(gq[64:], nh)])[None]
    gk = P['mla_k_norm'][l]
    gmk = jnp.concatenate([jnp.tile(gk[:64], nh), jnp.tile(gk[64:], nh)])[None]
    wukv = P['mla_w_ukv'][l]
    wuk = wukv[:, :, :64].reshape(-1, nh * 64)
    wuv = wukv[:, :, 64:].reshape(-1, nh * 64)
    return dict(
        gmix=P['norm_mix'][l][None], w_in=w_in_p, g64=g64, fb=fb,
        gcq=P['mla_q_lora_norm'][l][None], wuq=wuq_p.astype(BF16), gmq=gmq,
        glat=P['mla_kv_lora_norm'][l][None], wukv=jnp.concatenate([wuk, wuv], axis=1).astype(BF16),
        wuk=wuk.astype(BF16), wuv=wuv.astype(BF16), gmk=gmk,
        s64=_seg_ones(128, 64, 128, 64), s256=_seg_ones(256, 64, 256, 64), m96=_mla_head_matrix(), snr=_seg_ones(256, 64, 128, 32),
        lam=P['df_lambda'][l], gdf=P['df_out_norm'][l][None],
        w_gate=P['w_gate'][l].astype(BF16), w_branch=P['w_branch'][l].astype(BF16),
        w_out=P['w_out'][l].astype(BF16),
        gffn=P['norm_ffn'][l][None],
        w_group=_hi_lo(jnp.pad(P['moe_w_group'][l], ((0, 0), (0, LANES - N_GROUPS)))),
        b_group=jnp.pad(P['moe_b_group'][l], (0, LANES - N_GROUPS))[None],
        w_router=_hi_lo(jnp.pad(P['moe_w_router'][l], ((0, 0), (0, LANES - N_GROUPS * EXPERTS_PER_GROUP)))),
        b_router=jnp.pad(P['moe_b_router'][l], (0, LANES - N_GROUPS * EXPERTS_PER_GROUP))[None],
        w_gate_up=P['moe_w_gate_up'][l].astype(BF16), w_down=P['moe_w_down'][l].astype(BF16),
    )


def _rope_table(pos):
    def cs(seg):
        half = seg // 2
        inv = ROPE_THETA ** (-jnp.arange(half, dtype=F32) / half)
        ang = pos.astype(F32)[:, None] * inv[None, :]
        c, s = jnp.cos(ang), jnp.sin(ang)
        reps = LANES // seg
        return jnp.tile(jnp.concatenate([c, c], axis=1), (1, reps)), jnp.tile(jnp.concatenate([-s, s], axis=1), (1, reps))
    c64, s64 = cs(64)
    c32, s32 = cs(32)
    return jnp.concatenate([c64, s64, c32, s32], axis=1)


def _rows(outs, B, L):
    sbq, sbkv, dfq, dfk, dfv, fxq, fxkv, misc, mq, lat, mk, mv = outs
    return (sbkv[:, :L].reshape(B, L, 2, 2, 64), dfk[:, :L].reshape(B, L, 2, 2, 64),
            dfv[:, :L].reshape(B, L, 2, 128), fxkv[:, :L].reshape(B, L, 2, 2, 64),
            misc[:, :L, 32:36], lat[:, :L], misc[:, :L, 0:32])


def _heads(a, n_outer, d):
    B, Lp, _ = a.shape
    return a.reshape(B, Lp, n_outer, d).transpose(0, 2, 1, 3)


def _unheads(o):
    B, H, G, Lp, d = o.shape
    return o.transpose(0, 3, 1, 2, 4).reshape(B * Lp, H * G * d)


def _prompt_layer(x, l, lw, tab, L):
    B, Lp, D = x.shape
    outs = _proj_call(x, tab, lw, TQ)
    sbq, sbkv, dfq, dfk, dfv, fxq, fxkv, misc, mq, lat, mk, mv = outs

    def qh(a, n_outer, g, d):
        return a.astype(BF16).reshape(B, Lp, n_outer, g, d).transpose(0, 2, 3, 1, 4)

    def kh(a, n_outer, d):
        return _heads(a.astype(BF16), n_outer, d)

    sb_kv = kh(sbkv, 4, 64)
    sb_o = _sb_call(qh(sbq, 2, 2, 64), sb_kv[:, 0:2], sb_kv[:, 2:4], 64 ** -0.5)
    df_o = _flash_call(qh(dfq, 4, 2, 64), kh(dfk, 4, 64), kh(dfv, 2, 128), 64 ** -0.5)
    logf = misc[:, :, 32:36].transpose(0, 2, 1).reshape(B * 4, Lp)
    cum = _cumsum_call(logf).reshape(B, 2, 2, Lp)
    fx_kv = kh(fxkv, 4, 64)
    fx_o = _flash_call(qh(fxq, 2, 2, 64), fx_kv[:, 0:2], fx_kv[:, 2:4], 64 ** -0.5, ck=cum[:, :, :, None, :])
    ml_q = jnp.concatenate([qh(mq[..., 0:256], 4, 1, 64), qh(mq[..., 256:384], 4, 1, 32)], axis=-1)
    ml_k = jnp.concatenate([kh(mk[..., 0:256], 4, 64), kh(mk[..., 256:384], 4, 32)], axis=-1)
    ml_o = _flash_call(ml_q, ml_k, kh(mv, 4, 64), 96 ** -0.5)

    lam_init = 0.8 - 0.6 * math.exp(-0.3 * l)
    xf = _merge_call(x.reshape(B * Lp, D), _unheads(sb_o), _unheads(df_o[:, 0:2]), _unheads(df_o[:, 2:4]),
                     _unheads(fx_o), _unheads(ml_o), lw, lam_init, TM_MERGE)
    xf = _moe_call(xf, lw, TM_MOE)
    return xf.reshape(B, Lp, D), _rows(outs, B, L)


def _block_diag_rows(q, n_heads, g, d):
    N = q.shape[0]
    q = q.reshape(N, n_heads, g, d)
    eye = jnp.eye(n_heads, dtype=q.dtype)
    out = jnp.einsum('nhgd,hk->nhgkd', q, eye).reshape(N, n_heads * g, n_heads * d)
    return jnp.pad(out, ((0, 0), (0, SUBLANES - n_heads * g), (0, 0)))


def _sample_layer(xs, l, lw, tab, page_table, caches, ind):
    N, D = xs.shape
    outs = _proj_call(xs[None], tab, lw, N)
    sbq, sbkv, dfq, dfk, dfv, fxq, fxkv, misc, mq, lat, mk, mv = [o[0] for o in outs]
    qsb = _block_diag_rows(sbq, 2, 2, 64)
    qdf = _block_diag_rows(dfq, 4, 2, 64)
    qfx = _block_diag_rows(fxq, 2, 2, 64)
    qmn = _block_diag_rows(mq[:, 0:256], 4, 1, 64)
    qmr = jnp.pad(mq[:, 256:384].reshape(N, 4, 32), ((0, 0), (0, 4), (0, 0)))
    new = (dfk[:, None], dfv[:, None], fxkv[:, None], lat[:, None], misc[:, None])
    u = jnp.asarray(np.tril(np.ones((128, 128), np.float32), -1), dtype=BF16)
    consts = (u, lw['wuk'], lw['wuv'], lw['gmk'], ind)
    osb, odf, ofx, oml = _decode_call(l, page_table, (qsb, qdf, qfx, qmn, qmr), new, consts, caches)
    sb_o = jnp.concatenate([osb[:, r, (r // 2) * 64:(r // 2 + 1) * 64] for r in range(4)], axis=-1)
    df0 = jnp.concatenate([odf[:, r, (r // 2) * 128:(r // 2 + 1) * 128] for r in range(4)], axis=-1)
    df1 = jnp.concatenate([odf[:, 4 + r, (r // 2) * 128:(r // 2 + 1) * 128] for r in range(4)], axis=-1)
    fx_o = jnp.concatenate([ofx[:, r, (r // 2) * 64:(r // 2 + 1) * 64] for r in range(4)], axis=-1)
    ml_o = jnp.concatenate([oml[:, r, r * 64:(r + 1) * 64] for r in range(4)], axis=-1)
    lam_init = 0.8 - 0.6 * math.exp(-0.3 * l)
    xf = _merge_call(xs, sb_o, df0, df1, fx_o, ml_o, lw, lam_init, N)
    xf = _moe_call(xf, lw, N)
    return xf, _rows([o[None] for o in (sbq, sbkv, dfq, dfk, dfv, fxq, fxkv, misc, mq, lat, mk, mv)], 1, N)


def kernel(x_prompt, x_sample, cache_sb_kv, cache_df_k, cache_df_v, cache_fx_kv, cache_fx_logf, cache_mla_latent, cache_mla_krope, page_table, meta_tokens, norm_mix, norm_ffn, w_in, w_gate, w_branch, w_out, df_q_norm, df_k_norm, df_lambda, df_out_norm, fx_q_norm, fx_k_norm, fx_forget_bias, mla_q_lora_norm, mla_kv_lora_norm, mla_w_uq, mla_w_ukv, mla_q_norm, mla_k_norm, moe_w_group, moe_b_group, moe_w_router, moe_b_router, moe_w_gate_up, moe_w_down):
    P = dict(norm_mix=norm_mix, norm_ffn=norm_ffn, w_in=w_in, w_gate=w_gate, w_branch=w_branch, w_out=w_out,
             df_q_norm=df_q_norm, df_k_norm=df_k_norm, df_lambda=df_lambda, df_out_norm=df_out_norm,
             fx_q_norm=fx_q_norm, fx_k_norm=fx_k_norm, fx_forget_bias=fx_forget_bias,
             mla_q_lora_norm=mla_q_lora_norm, mla_kv_lora_norm=mla_kv_lora_norm, mla_w_uq=mla_w_uq,
             mla_w_ukv=mla_w_ukv, mla_q_norm=mla_q_norm, mla_k_norm=mla_k_norm, moe_w_group=moe_w_group,
             moe_b_group=moe_b_group, moe_w_router=moe_w_router, moe_b_router=moe_b_router,
             moe_w_gate_up=moe_w_gate_up, moe_w_down=moe_w_down)
    depth = w_in.shape[0]
    B, S, D = x_prompt.shape
    L = S + N_META
    Lp = -(-L // TQ) * TQ
    lws = [_layer_weights(l, P) for l in range(depth)]

    meta = jnp.broadcast_to(meta_tokens[None].astype(x_prompt.dtype), (B, N_META, D))
    x = jnp.concatenate([meta, x_prompt, jnp.zeros((B, Lp - L, D), x_prompt.dtype)], axis=1)
    tab_p = _rope_table(jnp.arange(Lp, dtype=jnp.int32))
    rows_p = []
    for l in range(depth):
        x, rows = _prompt_layer(x, l, lws[l], tab_p, L)
        rows_p.append(rows)
    y_prompt = x[:, N_META:L]

    nb, dec_seq, _ = x_sample.shape
    assert dec_seq == 1
    n_pool, page = cache_sb_kv.shape[1], cache_sb_kv.shape[2]
    past_len = page_table.shape[1] * page
    tab_s = _rope_table(jnp.full((nb,), past_len, dtype=jnp.int32))
    assert page == LANES and n_pool % 2 == 0
    key_minor = (0, 1, 3, 4, 5, 2)
    caches = (cache_sb_kv.transpose(key_minor), cache_df_k.transpose(key_minor),
              cache_df_v.reshape(depth, n_pool, 2 * page, 128), cache_fx_kv.transpose(key_minor),
              cache_mla_latent, cache_mla_krope.transpose(0, 1, 3, 2),
              cache_fx_logf.transpose(0, 1, 3, 2).reshape(depth, n_pool // 2, SUBLANES, LANES))
    ind = _seg_ones(SUBLANES, 1, 256, 64)
    xs = x_sample.reshape(nb, D)
    rows_s = []
    for l in range(depth):
        xs, rows = _sample_layer(xs, l, lws[l], tab_s, page_table, caches, ind)
        rows_s.append(tuple(r.reshape((nb, 1) + r.shape[2:]) for r in rows))
    y_sample = xs.reshape(nb, 1, D)

    def stack(rl):
        return tuple(jnp.stack([r[i] for r in rl], axis=0) for i in range(len(rl[0])))

    sb_kv_p, df_k_p, df_v_p, fx_kv_p, fx_logf_p, mla_lat_p, mla_kr_p = stack(rows_p)
    sb_kv_s, df_k_s, df_v_s, fx_kv_s, fx_logf_s, mla_lat_s, mla_kr_s = stack(rows_s)
    return (y_prompt, y_sample, sb_kv_p, sb_kv_s, df_k_p, df_k_s, df_v_p, df_v_s, fx_kv_p, fx_kv_s,
            fx_logf_p, fx_logf_s, mla_lat_p, mla_lat_s, mla_kr_p, mla_kr_s)
```

```python
import functools
import math

import numpy as np
import jax
import jax.numpy as jnp
from jax import lax
from jax.experimental import pallas as pl
from jax.experimental.pallas import tpu as pltpu

F32 = jnp.float32
BF16 = jnp.bfloat16

N_META = 16
ROPE_THETA = 10000.0
EPS = 1e-6
NEG = -1e30
FX_FORGET_BIAS = 5.0
N_BRANCH = 4
N_GROUPS = 4
EXPERTS_PER_GROUP = 4

LANES = 128
SUBLANES = 8
VMEM_LIMIT = 56 * 1024 * 1024

TQ = 256
ROW_CHUNK = 512
KEY_BLOCK_GROUPS = (4, 2, 1)
TM_MERGE = 256
TM_MOE = 512
PAGES_PER_STEP = 32
N_CACHES = 7
LF_CACHE = 6


def _cparams(sem):
    return pltpu.CompilerParams(dimension_semantics=sem, vmem_limit_bytes=VMEM_LIMIT)


def _dot(a, b):
    return jnp.dot(a, b, preferred_element_type=F32)


def _dot_nt(a, b):
    return lax.dot_general(a, b, (((1,), (1,)), ((), ())), preferred_element_type=F32)


def _split2(x):
    hi = x.astype(BF16)
    lo = (x - hi.astype(F32)).astype(BF16)
    return hi, lo


def _dot_hilo(x, w):
    hi, lo = _split2(x)
    return _dot(hi, w) + _dot(lo, w)


def _split3(x):
    a = x.astype(BF16)
    r = x - a.astype(F32)
    b = r.astype(BF16)
    c = (r - b.astype(F32)).astype(BF16)
    return a, b, c


def _dot_3way(x, w):
    a, b, c = _split3(x)
    return _dot(a, w) + _dot(b, w) + _dot(c, w)


def _log_sigmoid(z):
    return jnp.minimum(z, 0.0) - jnp.log1p(jnp.exp(-jnp.abs(z)))


def _lane_iota(shape):
    return lax.broadcasted_iota(jnp.int32, shape, len(shape) - 1)


def _swap_halves(y, seg):
    half = seg // 2
    lane = _lane_iota(y.shape)
    fwd = pltpu.roll(y, LANES - half, 1)
    bwd = pltpu.roll(y, half, 1)
    return jnp.where((lane % seg) < half, fwd, bwd)


def _seg_ones(n_rows, row_seg, n_cols, col_seg):
    r = np.arange(n_rows)[:, None] // row_seg
    c = np.arange(n_cols)[None, :] // col_seg
    return jnp.asarray((r == c).astype(np.float32), dtype=BF16)


def _mla_head_matrix():
    head = np.concatenate([np.arange(256) // 64, np.arange(128) // 32])
    return jnp.asarray((head[:, None] == head[None, :]).astype(np.float32), dtype=BF16)


def _proj_kernel(x_ref, tab_ref, gmix_ref, w_ref, g64_ref, fb_ref, gcq_ref, wuq_ref, gmq_ref,
                 glat_ref, wukv_ref, gmk_ref, s64_ref, s256_ref, m96_ref, snr_ref,
                 sbq_ref, sbkv_ref, dfq_ref, dfk_ref, dfv_ref, fxq_ref, fxkv_ref, misc_ref,
                 mq_ref, lat_ref, mk_ref, mv_ref):
    x = x_ref[...]
    h = x * lax.rsqrt(jnp.mean(x * x, axis=-1, keepdims=True) + EPS) * gmix_ref[...]
    hb = h.astype(BF16)
    tab = tab_ref[...]
    cos64, sin64 = tab[:, 0:128], tab[:, 128:256]
    cos32, sin32 = tab[:, 256:384], tab[:, 384:512]
    s64 = s64_ref[...]
    g64 = g64_ref[...]

    def proj(a, b):
        return _dot(hb, w_ref[:, a:b])

    def segnorm64(y, g):
        ms = _dot_hilo(y * y, s64) * (1.0 / 64.0)
        return y * lax.rsqrt(ms + EPS) * g

    def rope64(y):
        return y * cos64 + _swap_halves(y, 64) * sin64

    sbq_ref[...] = proj(0, 256)
    sbkv_ref[...] = proj(256, 512)
    for c in range(4):
        y = proj(512 + 128 * c, 640 + 128 * c)
        dfq_ref[:, 128 * c:128 * (c + 1)] = rope64(segnorm64(y, g64[0:1]))
    for c in range(2):
        y = proj(1024 + 128 * c, 1152 + 128 * c)
        dfk_ref[:, 128 * c:128 * (c + 1)] = rope64(segnorm64(y, g64[1:2]))
    dfv_ref[...] = proj(1280, 1536)
    for c in range(2):
        y = proj(1536 + 128 * c, 1664 + 128 * c)
        fxq_ref[:, 128 * c:128 * (c + 1)] = segnorm64(y, g64[2:3])
    fxkv_ref[:, 0:128] = segnorm64(proj(1792, 1920), g64[3:4])
    fxkv_ref[:, 128:256] = proj(1920, 2048)

    ym = proj(2432, 2560)
    lane = _lane_iota(ym.shape)
    kr = ym * cos32 + _swap_halves(ym, 32) * sin32
    logf = _log_sigmoid(ym + fb_ref[...])
    misc = jnp.where(lane < 32, kr, jnp.where(lane < 36, logf, 0.0))
    misc_ref[...] = misc

    cq = proj(2048, 2304)
    cq = cq * lax.rsqrt(jnp.mean(cq * cq, axis=-1, keepdims=True) + EPS) * gcq_ref[...]
    qf = _dot(cq.astype(BF16), wuq_ref[...])
    qr = qf[:, 256:384]
    qr = qr * cos32 + _swap_halves(qr, 32) * sin32
    qf = jnp.concatenate([qf[:, 0:256], qr], axis=1)
    ms = _dot_hilo(qf * qf, m96_ref[...]) * (1.0 / 96.0)
    mq_ref[...] = qf * lax.rsqrt(ms + EPS) * gmq_ref[...]

    ckv = proj(2304, 2432)
    lat = ckv * lax.rsqrt(jnp.mean(ckv * ckv, axis=-1, keepdims=True) + EPS) * glat_ref[...]
    lat_ref[...] = lat

    kv = _dot(lat.astype(BF16), wukv_ref[...])
    kn = kv[:, 0:256]
    mv_ref[...] = kv[:, 256:512]
    krm = jnp.where(lane < 32, kr, 0.0)
    n2 = jnp.sum(krm * krm, axis=-1, keepdims=True)
    kn2h, kn2l = _split2(kn * kn)
    s256 = s256_ref[...]
    n1 = _dot(kn2h, s256) + _dot(kn2l, s256)
    n1r = _dot(kn2h, snr_ref[...]) + _dot(kn2l, snr_ref[...])
    gmk = gmk_ref[...]
    mk_ref[:, 0:256] = kn * lax.rsqrt((n1 + n2) * (1.0 / 96.0) + EPS) * gmk[:, 0:256]
    krt = krm + pltpu.roll(krm, 32, 1) + pltpu.roll(krm, 64, 1) + pltpu.roll(krm, 96, 1)
    mk_ref[:, 256:384] = krt * lax.rsqrt((n1r + n2) * (1.0 / 96.0) + EPS) * gmk[:, 256:384]


def _proj_call(x, tab, lw, tm):
    B, Lp, D = x.shape
    nl = Lp // tm
    widths = (256, 256, 512, 256, 256, 256, 256, 128, 384, 128, 384, 256)

    def tok(w):
        return pl.BlockSpec((None, tm, w), lambda b, i: (b, i, 0))

    def full(a):
        return pl.BlockSpec(a.shape, lambda b, i: (0,) * a.ndim)

    consts = (lw['gmix'], lw['w_in'], lw['g64'], lw['fb'], lw['gcq'], lw['wuq'], lw['gmq'], lw['glat'],
              lw['wukv'], lw['gmk'], lw['s64'], lw['s256'], lw['m96'], lw['snr'])
    return pl.pallas_call(
        _proj_kernel,
        grid=(B, nl),
        in_specs=[tok(D), pl.BlockSpec((tm, 512), lambda b, i: (i, 0))] + [full(a) for a in consts],
        out_specs=[tok(w) for w in widths],
        out_shape=[jax.ShapeDtypeStruct((B, Lp, w), F32) for w in widths],
        compiler_params=_cparams(("parallel", "parallel")),
        name="proj",
    )(x, tab, *consts)


def _lane_tile(x, width):
    return x if width == LANES else jnp.concatenate([x] * (width // LANES), axis=1)


def _q_rows(q_ref, r0, ch):
    G, tq, d = q_ref.shape
    if ch <= tq:
        g, q0 = divmod(r0, tq)
        return q_ref[g, q0:q0 + ch, :]
    return q_ref[r0 // tq:(r0 + ch) // tq].reshape(ch, d)


def _rows_of_heads(per_head, r0, ch, tq):
    if ch <= tq:
        return per_head[r0 // tq]
    n = per_head[0].shape[-1]
    return jnp.concatenate([jnp.broadcast_to(per_head[g], (tq, n))
                            for g in range(r0 // tq, (r0 + ch) // tq)], axis=0)


def _flash_kernel(*refs, G, tq, scale, use_bias):
    if use_bias:
        q_ref, k_ref, v_ref, ck_ref, o_ref, m_sc, l_sc, acc_sc = refs
    else:
        q_ref, k_ref, v_ref, o_ref, m_sc, l_sc, acc_sc = refs
    i = pl.program_id(2)
    dv = v_ref.shape[-1]
    rows = G * tq
    ch = min(ROW_CHUNK, rows)
    m_sc[...] = jnp.full((rows, LANES), NEG, F32)
    l_sc[...] = jnp.zeros((rows, LANES), F32)
    acc_sc[...] = jnp.zeros((rows, dv), F32)
    if use_bias:
        c0 = ck_ref[:, :, pl.ds(pl.multiple_of(i * tq, tq), LANES)][:, :, 0:1]

    def step(j, width, masked):
        start = pl.multiple_of(j * tq, tq)
        k = k_ref[pl.ds(start, width), :]
        v = v_ref[pl.ds(start, width), :]
        if use_bias:
            bias = c0 - ck_ref[:, :, pl.ds(start, width)]
        for r0 in range(0, rows, ch):
            s = _dot_nt(_q_rows(q_ref, r0, ch), k) * scale
            if use_bias:
                s = s + _rows_of_heads([bias[g] for g in range(G)], r0, ch, tq)
            if masked:
                qpos = (r0 + lax.broadcasted_iota(jnp.int32, (ch, width), 0)) % tq
                kpos = lax.broadcasted_iota(jnp.int32, (ch, width), 1)
                s = jnp.where(kpos <= qpos, s, NEG)
            m_old = m_sc[r0:r0 + ch, :]
            m_new = jnp.maximum(m_old, jnp.max(s, axis=-1, keepdims=True))
            p = jnp.exp(s - _lane_tile(m_new, width))
            alpha = jnp.exp(m_old - m_new)
            l_sc[r0:r0 + ch, :] = alpha * l_sc[r0:r0 + ch, :] + jnp.sum(p, axis=-1, keepdims=True)
            acc_sc[r0:r0 + ch, :] = alpha[:, 0:dv] * acc_sc[r0:r0 + ch, :] + _dot(p.astype(BF16), v)
            m_sc[r0:r0 + ch, :] = m_new

    base = 0
    for kb in KEY_BLOCK_GROUPS:
        n = (i - base) // kb

        def body(jj, c, kb=kb, base=base):
            step(base + kb * jj, kb * tq, False)
            return c

        lax.fori_loop(0, n, body, 0)
        base = base + kb * n
    step(i, tq, True)
    _store_heads(o_ref, acc_sc[...] / l_sc[:, 0:dv], G, tq)


def _store_heads(o_ref, o, G, tq):
    if len(o_ref.shape) == 3:
        o_ref[...] = o.reshape(G, tq, o.shape[-1])
    else:
        o_ref[...] = jnp.concatenate([o[g * tq:(g + 1) * tq] for g in range(G)], axis=1)


def _attn_out(B, H, G, Lp, dv, token_major):
    if token_major:
        assert (G * dv) % LANES == 0
        return (pl.BlockSpec((None, TQ, G * dv), lambda b, h, i: (b, i, h)),
                jax.ShapeDtypeStruct((B, Lp, H * G * dv), F32))
    return (pl.BlockSpec((None, None, G, TQ, dv), lambda b, h, i: (b, h, 0, i, 0)),
            jax.ShapeDtypeStruct((B, H, G, Lp, dv), F32))


def _flash_call(q, k, v, scale, ck=None, token_major=True):
    B, H, G, Lp, dq = q.shape
    Hv, dv = v.shape[1], v.shape[-1]
    nq = Lp // TQ
    use_bias = ck is not None
    out_spec, out_shape = _attn_out(B, H, G, Lp, dv, token_major)
    in_specs = [
        pl.BlockSpec((None, None, G, TQ, dq), lambda b, h, i: (b, h, 0, i, 0)),
        pl.BlockSpec((None, None, Lp, dq), lambda b, h, i: (b, h, 0, 0)),
        pl.BlockSpec((None, None, Lp, dv), lambda b, h, i: (b, h % Hv, 0, 0)),
    ]
    args = [q, k, v]
    if use_bias:
        in_specs.append(pl.BlockSpec((None, None, G, 1, Lp), lambda b, h, i: (b, h, 0, 0, 0)))
        args.append(ck)
    return pl.pallas_call(
        functools.partial(_flash_kernel, G=G, tq=TQ, scale=scale, use_bias=use_bias),
        grid=(B, H, nq),
        in_specs=in_specs,
        out_specs=out_spec,
        out_shape=out_shape,
        scratch_shapes=[pltpu.VMEM((G * TQ, LANES), F32), pltpu.VMEM((G * TQ, LANES), F32),
                        pltpu.VMEM((G * TQ, dv), F32)],
        compiler_params=_cparams(("parallel", "parallel", "arbitrary")),
        name="flash_bias" if use_bias else "flash",
    )(*args)


def _sb_kernel(q_ref, k_ref, v_ref, u_ref, o_ref, r_sc, acc_sc, *, G, tq, scale):
    i = pl.program_id(2)
    dq = q_ref.shape[-1]
    rows = G * tq
    ch = min(ROW_CHUNK, rows)
    r_sc[...] = jnp.zeros((rows, LANES), F32)
    acc_sc[...] = jnp.zeros((rows, dq), F32)

    def step(j, nblk, masked):
        start = pl.multiple_of(j * tq, tq)
        k = k_ref[pl.ds(start, nblk * tq), :]
        v = v_ref[pl.ds(start, nblk * tq), :]
        u = u_ref[...]
        for r0 in range(0, rows, ch):
            z = _dot_nt(_q_rows(q_ref, r0, ch), k) * scale
            ls = _log_sigmoid(z)
            lk = ls - z
            if masked:
                qpos = (r0 + lax.broadcasted_iota(jnp.int32, (ch, tq), 0)) % tq
                kpos = lax.broadcasted_iota(jnp.int32, (ch, tq), 1)
                vis = kpos < qpos
                lk = jnp.where(vis, lk, 0.0)
            r = _lane_tile(r_sc[r0:r0 + ch, :], tq)
            later = [None] * nblk
            for bk in reversed(range(nblk)):
                lkb = lk[:, bk * tq:(bk + 1) * tq]
                later[bk] = _dot_hilo(lkb, u) + r
                r = r + jnp.sum(lkb, axis=-1, keepdims=True)
            w = jnp.exp(ls + (later[0] if nblk == 1 else jnp.concatenate(later, axis=1)))
            if masked:
                w = jnp.where(vis, w, 0.0)
            acc_sc[r0:r0 + ch, :] += _dot(w.astype(BF16), v)
            r_sc[r0:r0 + ch, :] = r[:, 0:LANES]

    step(i, 1, True)
    top = i
    for kb in KEY_BLOCK_GROUPS:
        n = top // kb

        def body(t, c, kb=kb, top=top):
            step(top - kb * (t + 1), kb, False)
            return c

        lax.fori_loop(0, n, body, 0)
        top = top - kb * n
    _store_heads(o_ref, acc_sc[...], G, tq)


def _sb_call(q, k, v, scale):
    B, H, G, Lp, dq = q.shape
    nq = Lp // TQ
    u = jnp.asarray(np.tril(np.ones((TQ, TQ), np.float32), -1), dtype=BF16)
    out_spec, out_shape = _attn_out(B, H, G, Lp, dq, True)
    return pl.pallas_call(
        functools.partial(_sb_kernel, G=G, tq=TQ, scale=scale),
        grid=(B, H, nq),
        in_specs=[
            pl.BlockSpec((None, None, G, TQ, dq), lambda b, h, i: (b, h, 0, i, 0)),
            pl.BlockSpec((None, None, Lp, dq), lambda b, h, i: (b, h, 0, 0)),
            pl.BlockSpec((None, None, Lp, dq), lambda b, h, i: (b, h, 0, 0)),
            pl.BlockSpec((TQ, TQ), lambda b, h, i: (0, 0)),
        ],
        out_specs=out_spec,
        out_shape=out_shape,
        scratch_shapes=[pltpu.VMEM((G * TQ, LANES), F32), pltpu.VMEM((G * TQ, dq), F32)],
        compiler_params=_cparams(("parallel", "parallel", "arbitrary")),
        name="sb",
    )(q, k, v, u)


def _cumsum_kernel(x_ref, u_ref, o_ref, *, blk):
    n = x_ref.shape[-1] // blk
    u = u_ref[...]
    carry = jnp.zeros((x_ref.shape[0], 1), F32)
    for j in range(n):
        x = x_ref[:, j * blk:(j + 1) * blk]
        c = _dot_3way(x, u) + carry
        o_ref[:, j * blk:(j + 1) * blk] = c
        carry = c[:, blk - 1:blk]


def _cumsum_call(x):
    R, Lp = x.shape
    blk = 256
    u = jnp.asarray(np.triu(np.ones((blk, blk), np.float32)), dtype=BF16)
    return pl.pallas_call(
        functools.partial(_cumsum_kernel, blk=blk),
        out_shape=jax.ShapeDtypeStruct((R, Lp), F32),
        name="logf_cumsum",
    )(x, u)


def _merge_kernel(x_ref, sb_ref, df0_ref, df1_ref, fx_ref, ml_ref, gmix_ref, lam_ref, gdf_ref,
                  wg_ref, wb_ref, wo_ref, o_ref, *, lam_init):
    x = x_ref[...]
    D = x.shape[-1]
    h = (x * lax.rsqrt(jnp.mean(x * x, axis=-1, keepdims=True) + EPS) * gmix_ref[...]).astype(BF16)
    lp = lam_ref[...]
    lam = (jnp.exp(jnp.sum(lp[0:1] * lp[1:2], axis=-1, keepdims=True))
           - jnp.exp(jnp.sum(lp[2:3] * lp[3:4], axis=-1, keepdims=True)) + lam_init)
    gdf = gdf_ref[...]
    dfs = []
    for c in range(4):
        a = df0_ref[:, 128 * c:128 * (c + 1)] - lam * df1_ref[:, 128 * c:128 * (c + 1)]
        a = a * lax.rsqrt(jnp.mean(a * a, axis=-1, keepdims=True) + EPS) * gdf
        dfs.append(a * (1.0 - lam_init))
    df = jnp.concatenate(dfs, axis=1)
    merged = jnp.zeros(x.shape, F32)
    off = 0
    for b, o in enumerate((sb_ref[...], df, fx_ref[...], ml_ref[...])):
        width = o.shape[-1]
        gate = jax.nn.sigmoid(_dot(h, wg_ref[:, b * D:(b + 1) * D]))
        merged = merged + gate * _dot(o.astype(BF16), wb_ref[off:off + width, :])
        off += width
    o_ref[...] = x + _dot(merged.astype(BF16), wo_ref[...])


def _merge_call(x, sb, df, fx, ml, lw, lam_init, tm):
    N, D = x.shape
    assert N % tm == 0

    def tok(w, col=0):
        return pl.BlockSpec((tm, w), lambda i: (i, col))

    def full(a):
        return pl.BlockSpec(a.shape, lambda i: (0,) * a.ndim)

    consts = (lw['gmix'], lw['lam'], lw['gdf'], lw['w_gate'], lw['w_branch'], lw['w_out'])
    return pl.pallas_call(
        functools.partial(_merge_kernel, lam_init=lam_init),
        grid=(N // tm,),
        in_specs=[tok(D), tok(256), tok(512, 0), tok(512, 1), tok(256), tok(256)] + [full(a) for a in consts],
        out_specs=tok(D),
        out_shape=jax.ShapeDtypeStruct((N, D), F32),
        compiler_params=_cparams(("parallel",)),
        name="merge",
    )(x, sb, df, df, fx, ml, *consts)


def _moe_kernel(x_ref, gffn_ref, wgr_ref, bgr_ref, wrt_ref, brt_ref, wgu_ref, wdn_ref, o_ref,
                h_sc, comb_sc, acc_sc):
    gi = pl.program_id(1)
    tm = x_ref.shape[0]
    hid = wdn_ref.shape[1]

    @pl.when(gi == 0)
    def _():
        x = x_ref[...]
        h, h_lo = _split2(x * lax.rsqrt(jnp.mean(x * x, axis=-1, keepdims=True) + EPS) * gffn_ref[...])
        h_sc[...] = h

        def logits(w_ref):
            return _dot(h, w_ref[0]) + (_dot(h_lo, w_ref[0]) + _dot(h, w_ref[1]))

        lane = _lane_iota((tm, LANES))
        big = jnp.int32(LANES)
        gl = jnp.where(lane < N_GROUPS, logits(wgr_ref) + bgr_ref[...], NEG)
        gmax = jnp.max(gl, axis=-1, keepdims=True)
        gidx = jnp.min(jnp.where(gl == gmax, lane, big), axis=-1, keepdims=True)
        g_w = 1.0 / jnp.sum(jnp.exp(gl - gmax), axis=-1, keepdims=True)
        sel = (lane // EXPERTS_PER_GROUP) == gidx
        el = jnp.where(sel, logits(wrt_ref) + brt_ref[...], NEG)
        emax = jnp.max(el, axis=-1, keepdims=True)
        pe = jnp.exp(el - emax)
        p = pe / jnp.sum(pe, axis=-1, keepdims=True)
        p = jnp.where(sel, p, -1.0)
        top1 = jnp.max(p, axis=-1, keepdims=True)
        i1 = jnp.min(jnp.where(p == top1, lane, big), axis=-1, keepdims=True)
        p2 = jnp.where(lane == i1, -1.0, p)
        top2 = jnp.max(p2, axis=-1, keepdims=True)
        i2 = jnp.min(jnp.where(p2 == top2, lane, big), axis=-1, keepdims=True)
        den = top1 + top2
        comb_sc[...] = (jnp.where(lane == i1, top1 / den * g_w, 0.0)
                        + jnp.where(lane == i2, top2 / den * g_w, 0.0))
        acc_sc[...] = jnp.zeros(acc_sc.shape, F32)

    h = h_sc[...]
    comb_hi, comb_lo = _split2(comb_sc[...])
    row = lax.broadcasted_iota(jnp.int32, (LANES, hid), 0)
    for e in range(EXPERTS_PER_GROUP):
        sel = jnp.where(row == gi * EXPERTS_PER_GROUP + e, 1.0, 0.0).astype(BF16)
        cb = _dot(comb_hi, sel) + _dot(comb_lo, sel)
        gu = _dot(h, wgu_ref[e])
        g = gu[:, :hid]
        act = g * jax.nn.sigmoid(g) * gu[:, hid:] * cb
        acc_sc[...] += _dot(act.astype(BF16), wdn_ref[e])

    @pl.when(gi == pl.num_programs(1) - 1)
    def _():
        o_ref[...] = x_ref[...] + acc_sc[...]


def _moe_call(x, lw, tm):
    N, D = x.shape
    assert N % tm == 0
    hid = lw['w_down'].shape[1]
    epg = EXPERTS_PER_GROUP

    def full(a):
        return pl.BlockSpec(a.shape, lambda i, g: (0,) * a.ndim)

    return pl.pallas_call(
        _moe_kernel,
        grid=(N // tm, N_GROUPS),
        in_specs=[pl.BlockSpec((tm, D), lambda i, g: (i, 0)),
                  full(lw['gffn']), full(lw['w_group']), full(lw['b_group']),
                  full(lw['w_router']), full(lw['b_router']),
                  pl.BlockSpec((epg, D, 2 * hid), lambda i, g: (g, 0, 0)),
                  pl.BlockSpec((epg, hid, D), lambda i, g: (g, 0, 0))],
        out_specs=pl.BlockSpec((tm, D), lambda i, g: (i, 0)),
        out_shape=jax.ShapeDtypeStruct((N, D), F32),
        scratch_shapes=[pltpu.VMEM((tm, D), BF16), pltpu.VMEM((tm, LANES), F32), pltpu.VMEM((tm, D), F32)],
        compiler_params=_cparams(("parallel", "arbitrary")),
        name="moe",
    )(x, lw['gffn'], lw['w_group'], lw['b_group'], lw['w_router'], lw['b_router'],
      lw['w_gate_up'], lw['w_down'])


def _decode_kernel(pt_ref, qsb_ref, qdf_ref, qfx_ref, qmn_ref, qmr_ref,
                   ndfk_ref, ndfv_ref, nfxkv_ref, nlat_ref, nmisc_ref,
                   u_ref, wuk_ref, wuv_ref, gmk_ref, ind_ref, *rest, npp, n_pages, layer):
    caches = rest[:N_CACHES]
    osb_ref, odf_ref, ofx_ref, oml_ref = rest[N_CACHES:N_CACHES + 4]
    (sb_acc, sb_car, df_m, df_l, df_acc, fx_m, fx_l, fx_acc, fx_car,
     ml_m, ml_l, ml_acc) = rest[N_CACHES + 4:N_CACHES + 16]
    bufs = rest[N_CACHES + 16:2 * N_CACHES + 16]
    sems = rest[2 * N_CACHES + 16]
    b = pl.program_id(0)
    c = pl.program_id(1)
    nsteps = pl.num_programs(1)
    R = SUBLANES
    t = b * nsteps + c
    slot = t % 2

    def page_copies(step, to_slot):
        sb_, sc_ = step // nsteps, step % nsteps
        out = []
        for p in range(npp):
            idx = pt_ref[sb_ * n_pages + (nsteps - 1 - sc_) * npp + p]
            for k in range(N_CACHES):
                src = caches[k].at[layer, idx // 2] if k == LF_CACHE else caches[k].at[layer, idx]
                out.append(pltpu.make_async_copy(src, bufs[k].at[to_slot, p], sems.at[to_slot, k]))
        return out

    @pl.when(t == 0)
    def _():
        for cp in page_copies(t, slot):
            cp.start()

    @pl.when(t + 1 < pl.num_programs(0) * nsteps)
    def _():
        for cp in page_copies(t + 1, 1 - slot):
            cp.start()

    for cp in page_copies(t, slot):
        cp.wait()
    pages = [bufs[k].at[slot, p] for p in range(npp) for k in range(N_CACHES)]

    @pl.when(c == 0)
    def _():
        sb_acc[...] = jnp.zeros(sb_acc.shape, F32)
        sb_car[...] = jnp.zeros(sb_car.shape, F32)
        for m_sc, l_sc, a_sc in ((df_m, df_l, df_acc), (fx_m, fx_l, fx_acc), (ml_m, ml_l, ml_acc)):
            m_sc[...] = jnp.full(m_sc.shape, NEG, F32)
            l_sc[...] = jnp.zeros(l_sc.shape, F32)
            a_sc[...] = jnp.zeros(a_sc.shape, F32)
        nm = nmisc_ref[...]
        rowi = lax.broadcasted_iota(jnp.int32, (R, LANES), 0)
        lanei = lax.broadcasted_iota(jnp.int32, (R, LANES), 1)
        pick = jnp.where(lanei == 32 + (rowi % 4), jnp.broadcast_to(nm, (R, LANES)), 0.0)
        fx_car[...] = jnp.sum(pick, axis=-1, keepdims=True)

    qsb = qsb_ref[...].astype(BF16)
    qdf = qdf_ref[...].astype(BF16)
    qfx = qfx_ref[...].astype(BF16)
    gmk = gmk_ref[...]
    qmn = (qmn_ref[...] * gmk[:, 0:256]).astype(BF16)
    qmr = (qmr_ref[...] * gmk[:, 256:288]).astype(BF16)
    ind = ind_ref[...]
    ones_r = jnp.ones((R, 32), BF16)
    wuk = wuk_ref[...]

    def rowsum(x):
        return jnp.sum(x, axis=-1, keepdims=True)

    sb_ls, sb_lk, df_s, fx_s, ml_s, lf_rows, latb = [], [], [], [], [], [], []
    for p in range(npp):
        sbT, dfkT, _, fxT, lat_ref, krT_ref, lf_ref = pages[7 * p:7 * p + 7]
        z = _dot(qsb, sbT[0].reshape(128, 128).astype(BF16)) * 0.125
        ls = _log_sigmoid(z)
        sb_ls.append(ls)
        sb_lk.append(ls - z)
        df_s.append(_dot(qdf, dfkT[...].reshape(256, 128).astype(BF16)) * 0.125)
        fx_s.append(_dot(qfx, fxT[0].reshape(128, 128).astype(BF16)) * 0.125)
        lb = lat_ref[...].astype(BF16)
        latb.append(lb)
        kn = _dot(lb, wuk)
        kr = krT_ref[...]
        a = _dot_nt(qmn, kn.astype(BF16)) + _dot(qmr, kr.astype(BF16))
        n = _dot_nt(ind, (kn * kn).astype(BF16)) + _dot(ones_r, (kr * kr).astype(BF16))
        ml_s.append(a * lax.rsqrt(n * (1.0 / 96.0) + EPS) * (96.0 ** -0.5))
        j = (nsteps - 1 - c) * npp + p
        odd = (pt_ref[b * n_pages + j] % 2) == 1
        blk = lf_ref[...]
        lf_rows.append(jnp.where(odd, pltpu.roll(blk, 4, 0), blk))

    pieces = []
    for p in range(npp):
        pieces += list(_split2(sb_lk[p])) + list(_split3(lf_rows[p]))
    suf = _dot(jnp.concatenate(pieces, axis=0), u_ref[...])
    sb_suf = [suf[5 * R * p:5 * R * p + R] + suf[5 * R * p + R:5 * R * p + 2 * R] for p in range(npp)]
    fx_suf = [suf[5 * R * p + 2 * R:5 * R * p + 3 * R] + suf[5 * R * p + 3 * R:5 * R * p + 4 * R]
              + suf[5 * R * p + 4 * R:5 * R * p + 5 * R] for p in range(npp)]

    car = sb_car[...]
    acc = sb_acc[...]
    for p in reversed(range(npp)):
        w = jnp.exp(sb_ls[p] + sb_suf[p] + car)
        acc = acc + _dot_nt(w.astype(BF16), pages[7 * p][1].reshape(128, 128).astype(BF16))
        car = car + rowsum(sb_lk[p])
    sb_acc[...] = acc
    sb_car[...] = car

    car = fx_car[...]
    for p in reversed(range(npp)):
        fx_s[p] = fx_s[p] + fx_suf[p] + car
        car = car + rowsum(lf_rows[p])
    fx_car[...] = car

    def softmax_step(m_sc, l_sc, a_sc, s_list, pv):
        m_old = m_sc[...]
        smax = s_list[0]
        for s in s_list[1:]:
            smax = jnp.maximum(smax, s)
        m_new = jnp.maximum(m_old, jnp.max(smax, axis=-1, keepdims=True))
        alpha = jnp.exp(m_old - m_new)
        ps = [jnp.exp(s - m_new) for s in s_list]
        psum = ps[0]
        for x in ps[1:]:
            psum = psum + x
        l_sc[...] = alpha * l_sc[...] + rowsum(psum)
        acc = alpha * a_sc[...]
        for p in range(npp):
            acc = acc + pv(p, ps[p].astype(BF16))
        a_sc[...] = acc
        m_sc[...] = m_new

    def df_pv(p, pb):
        dfv = pages[7 * p + 2]
        v0 = dfv[pl.ds(0, 128, stride=2), :].astype(BF16)
        v1 = dfv[pl.ds(1, 128, stride=2), :].astype(BF16)
        return jnp.concatenate([_dot(pb, v0), _dot(pb, v1)], axis=1)

    softmax_step(df_m, df_l, df_acc, df_s, df_pv)
    softmax_step(fx_m, fx_l, fx_acc, fx_s,
                 lambda p, pb: _dot_nt(pb, pages[7 * p + 3][1].reshape(128, 128).astype(BF16)))
    softmax_step(ml_m, ml_l, ml_acc, ml_s, lambda p, pb: _dot(pb, latb[p]))

    @pl.when(c == nsteps - 1)
    def _():
        def finish(m_sc, l_sc, a_sc, s_new, v_new):
            m_old = m_sc[...]
            m_new = jnp.maximum(m_old, s_new)
            p = jnp.exp(s_new - m_new)
            alpha = jnp.exp(m_old - m_new)
            return (alpha * a_sc[...] + p * v_new) / (alpha * l_sc[...] + p)

        osb_ref[...] = sb_acc[...]
        s_new = jnp.sum(qdf_ref[...] * ndfk_ref[...], axis=-1, keepdims=True) * 0.125
        odf_ref[...] = finish(df_m, df_l, df_acc, s_new, ndfv_ref[...])
        nfxkv = nfxkv_ref[...]
        s_new = jnp.sum(qfx_ref[...] * nfxkv[:, 0:128], axis=-1, keepdims=True) * 0.125
        ofx_ref[...] = finish(fx_m, fx_l, fx_acc, s_new, nfxkv[:, 128:256])
        nlat = nlat_ref[...]
        nlat8 = jnp.broadcast_to(nlat, (R, LANES))
        kn = _dot(nlat8.astype(BF16), wuk)
        krn = nmisc_ref[...][:, 0:32]
        indf = ind.astype(F32)
        a = (jnp.sum(qmn_ref[...] * gmk[:, 0:256] * kn, axis=-1, keepdims=True)
             + jnp.sum(qmr_ref[...] * gmk[:, 256:288] * krn, axis=-1, keepdims=True))
        n = (jnp.sum(indf * kn * kn, axis=-1, keepdims=True)
             + jnp.sum(krn * krn, axis=-1, keepdims=True))
        s_new = a * lax.rsqrt(n * (1.0 / 96.0) + EPS) * (96.0 ** -0.5)
        pl_ = finish(ml_m, ml_l, ml_acc, s_new, nlat8)
        oml_ref[...] = _dot(pl_.astype(BF16), wuv_ref[...])


def _decode_call(l, page_table, q, new, consts, caches):
    nb, n_pages = page_table.shape
    npp = PAGES_PER_STEP
    assert n_pages % npp == 0
    nsteps = n_pages // npp
    pt_flat = page_table.reshape(-1)

    def seq(a):
        return pl.BlockSpec((None,) + a.shape[1:], lambda b, c, pt: (b,) + (0,) * (a.ndim - 1))

    def full(a):
        return pl.BlockSpec(a.shape, lambda b, c, pt: (0,) * a.ndim)

    assert len(caches) == N_CACHES
    in_specs = ([seq(a) for a in q] + [seq(a) for a in new] + [full(a) for a in consts]
                + [pl.BlockSpec(memory_space=pl.ANY)] * N_CACHES)
    args = list(q) + list(new) + list(consts) + list(caches)
    out_widths = (128, 256, 128, 256)
    grid_spec = pltpu.PrefetchScalarGridSpec(
        num_scalar_prefetch=1,
        grid=(nb, nsteps),
        in_specs=in_specs,
        out_specs=[pl.BlockSpec((None, SUBLANES, w), lambda b, c, pt: (b, 0, 0)) for w in out_widths],
        scratch_shapes=[pltpu.VMEM((SUBLANES, 128), F32), pltpu.VMEM((SUBLANES, 1), F32),
                        pltpu.VMEM((SUBLANES, 1), F32), pltpu.VMEM((SUBLANES, 1), F32),
                        pltpu.VMEM((SUBLANES, 256), F32),
                        pltpu.VMEM((SUBLANES, 1), F32), pltpu.VMEM((SUBLANES, 1), F32),
                        pltpu.VMEM((SUBLANES, 128), F32), pltpu.VMEM((SUBLANES, 1), F32),
                        pltpu.VMEM((SUBLANES, 1), F32), pltpu.VMEM((SUBLANES, 1), F32),
                        pltpu.VMEM((SUBLANES, 128), F32)]
        + [pltpu.VMEM((2, npp) + a.shape[2:], a.dtype) for a in caches]
        + [pltpu.SemaphoreType.DMA((2, N_CACHES))],
    )
    return pl.pallas_call(
        functools.partial(_decode_kernel, npp=npp, n_pages=n_pages, layer=l),
        grid_spec=grid_spec,
        out_shape=[jax.ShapeDtypeStruct((nb, SUBLANES, w), F32) for w in out_widths],
        compiler_params=_cparams(("arbitrary", "arbitrary")),
        name="decode",
    )(pt_flat, *args)


def _hi_lo(w):
    return jnp.stack(_split2(w))


def _layer_weights(l, P):
    D = P['w_in'].shape[1]
    w_in = P['w_in'][l]
    w_in_p = jnp.concatenate([w_in[:, :2048], w_in[:, 2052:2436], w_in[:, 2436:2468], w_in[:, 2048:2052],
                              jnp.zeros((D, 92), F32)], axis=1).astype(BF16)
    t2 = lambda g: jnp.tile(g, 2)
    g64 = jnp.stack([t2(P['df_q_norm'][l]), t2(P['df_k_norm'][l]), t2(P['fx_q_norm'][l]), t2(P['fx_k_norm'][l])])
    g64 = jnp.concatenate([g64, jnp.zeros((4, 128), F32)], axis=0)
    fb = jnp.zeros((1, 128), F32).at[0, 32:36].set(P['fx_forget_bias'][l])
    wuq = P['mla_w_uq'][l]
    nh = wuq.shape[1]
    wuq_p = jnp.concatenate([wuq[:, :, :64].reshape(-1, nh * 64), wuq[:, :, 64:].reshape(-1, nh * 32)], axis=1)
    gq = P['mla_q_norm'][l]
    gmq = jnp.concatenate([jnp.tile(gq[:64], nh), jnp.tile(gq[64:], nh)])[None]
    gk = P['mla_k_norm'][l]
    gmk = jnp.concatenate([jnp.tile(gk[:64], nh), jnp.tile(gk[64:], nh)])[None]
    wukv = P['mla_w_ukv'][l]
    wuk = wukv[:, :, :64].reshape(-1, nh * 64)
    wuv = wukv[:, :, 64:].reshape(-1, nh * 64)
    return dict(
        gmix=P['norm_mix'][l][None], w_in=w_in_p, g64=g64, fb=fb,
        gcq=P['mla_q_lora_norm'][l][None], wuq=wuq_p.astype(BF16), gmq=gmq,
        glat=P['mla_kv_lora_norm'][l][None], wukv=jnp.concatenate([wuk, wuv], axis=1).astype(BF16),
        wuk=wuk.astype(BF16), wuv=wuv.astype(BF16), gmk=gmk,
        s64=_seg_ones(128, 64, 128, 64), s256=_seg_ones(256, 64, 256, 64), m96=_mla_head_matrix(), snr=_seg_ones(256, 64, 128, 32),
        lam=P['df_lambda'][l], gdf=P['df_out_norm'][l][None],
        w_gate=P['w_gate'][l].astype(BF16), w_branch=P['w_branch'][l].astype(BF16),
        w_out=P['w_out'][l].astype(BF16),
        gffn=P['norm_ffn'][l][None],
        w_group=_hi_lo(jnp.pad(P['moe_w_group'][l], ((0, 0), (0, LANES - N_GROUPS)))),
        b_group=jnp.pad(P['moe_b_group'][l], (0, LANES - N_GROUPS))[None],
        w_router=_hi_lo(jnp.pad(P['moe_w_router'][l], ((0, 0), (0, LANES - N_GROUPS * EXPERTS_PER_GROUP)))),
        b_router=jnp.pad(P['moe_b_router'][l], (0, LANES - N_GROUPS * EXPERTS_PER_GROUP))[None],
        w_gate_up=P['moe_w_gate_up'][l].astype(BF16), w_down=P['moe_w_down'][l].astype(BF16),
    )


def _rope_table(pos):
    def cs(seg):
        half = seg // 2
        inv = ROPE_THETA ** (-jnp.arange(half, dtype=F32) / half)
        ang = pos.astype(F32)[:, None] * inv[None, :]
        c, s = jnp.cos(ang), jnp.sin(ang)
        reps = LANES // seg
        return jnp.tile(jnp.concatenate([c, c], axis=1), (1, reps)), jnp.tile(jnp.concatenate([-s, s], axis=1), (1, reps))
    c64, s64 = cs(64)
    c32, s32 = cs(32)
    return jnp.concatenate([c64, s64, c32, s32], axis=1)


def _rows(outs, B, L):
    sbq, sbkv, dfq, dfk, dfv, fxq, fxkv, misc, mq, lat, mk, mv = outs
    return (sbkv[:, :L].reshape(B, L, 2, 2, 64), dfk[:, :L].reshape(B, L, 2, 2, 64),
            dfv[:, :L].reshape(B, L, 2, 128), fxkv[:, :L].reshape(B, L, 2, 2, 64),
            misc[:, :L, 32:36], lat[:, :L], misc[:, :L, 0:32])


def _heads(a, n_outer, d):
    B, Lp, _ = a.shape
    return a.reshape(B, Lp, n_outer, d).transpose(0, 2, 1, 3)


def _unheads(o):
    B, H, G, Lp, d = o.shape
    return o.transpose(0, 3, 1, 2, 4).reshape(B * Lp, H * G * d)


def _prompt_layer(x, l, lw, tab, L):
    B, Lp, D = x.shape
    outs = _proj_call(x, tab, lw, TQ)
    sbq, sbkv, dfq, dfk, dfv, fxq, fxkv, misc, mq, lat, mk, mv = outs

    def qh(a, n_outer, g, d):
        return a.astype(BF16).reshape(B, Lp, n_outer, g, d).transpose(0, 2, 3, 1, 4)

    def kh(a, n_outer, d):
        return _heads(a.astype(BF16), n_outer, d)

    sb_kv = kh(sbkv, 4, 64)
    sb_o = _sb_call(qh(sbq, 2, 2, 64), sb_kv[:, 0:2], sb_kv[:, 2:4], 64 ** -0.5)
    df_o = _flash_call(qh(dfq, 4, 2, 64), kh(dfk, 4, 64), kh(dfv, 2, 128), 64 ** -0.5)
    logf = misc[:, :, 32:36].transpose(0, 2, 1).reshape(B * 4, Lp)
    cum = _cumsum_call(logf).reshape(B, 2, 2, Lp)
    fx_kv = kh(fxkv, 4, 64)
    fx_o = _flash_call(qh(fxq, 2, 2, 64), fx_kv[:, 0:2], fx_kv[:, 2:4], 64 ** -0.5, ck=cum[:, :, :, None, :])
    ml_q = jnp.concatenate([qh(mq[..., 0:256], 4, 1, 64), qh(mq[..., 256:384], 4, 1, 32)], axis=-1)
    ml_k = jnp.concatenate([kh(mk[..., 0:256], 4, 64), kh(mk[..., 256:384], 4, 32)], axis=-1)
    ml_o = _flash_call(ml_q, ml_k, kh(mv, 4, 64), 96 ** -0.5, token_major=False)

    lam_init = 0.8 - 0.6 * math.exp(-0.3 * l)
    N = B * Lp
    xf = _merge_call(x.reshape(N, D), sb_o.reshape(N, 256), df_o.reshape(N, 1024), fx_o.reshape(N, 256),
                     _unheads(ml_o), lw, lam_init, TM_MERGE)
    xf = _moe_call(xf, lw, TM_MOE)
    return xf.reshape(B, Lp, D), _rows(outs, B, L)


def _block_diag_rows(q, n_heads, g, d):
    N = q.shape[0]
    q = q.reshape(N, n_heads, g, d)
    eye = jnp.eye(n_heads, dtype=q.dtype)
    out = jnp.einsum('nhgd,hk->nhgkd', q, eye).reshape(N, n_heads * g, n_heads * d)
    return jnp.pad(out, ((0, 0), (0, SUBLANES - n_heads * g), (0, 0)))


def _sample_layer(xs, l, lw, tab, page_table, caches, ind):
    N, D = xs.shape
    outs = _proj_call(xs[None], tab, lw, N)
    sbq, sbkv, dfq, dfk, dfv, fxq, fxkv, misc, mq, lat, mk, mv = [o[0] for o in outs]
    qsb = _block_diag_rows(sbq, 2, 2, 64)
    qdf = _block_diag_rows(dfq, 4, 2, 64)
    qfx = _block_diag_rows(fxq, 2, 2, 64)
    qmn = _block_diag_rows(mq[:, 0:256], 4, 1, 64)
    qmr = jnp.pad(mq[:, 256:384].reshape(N, 4, 32), ((0, 0), (0, 4), (0, 0)))
    new = (dfk[:, None], dfv[:, None], fxkv[:, None], lat[:, None], misc[:, None])
    u = jnp.asarray(np.tril(np.ones((128, 128), np.float32), -1), dtype=BF16)
    consts = (u, lw['wuk'], lw['wuv'], lw['gmk'], ind)
    osb, odf, ofx, oml = _decode_call(l, page_table, (qsb, qdf, qfx, qmn, qmr), new, consts, caches)
    sb_o = jnp.concatenate([osb[:, r, (r // 2) * 64:(r // 2 + 1) * 64] for r in range(4)], axis=-1)
    df0 = jnp.concatenate([odf[:, r, (r // 2) * 128:(r // 2 + 1) * 128] for r in range(4)], axis=-1)
    df1 = jnp.concatenate([odf[:, 4 + r, (r // 2) * 128:(r // 2 + 1) * 128] for r in range(4)], axis=-1)
    fx_o = jnp.concatenate([ofx[:, r, (r // 2) * 64:(r // 2 + 1) * 64] for r in range(4)], axis=-1)
    ml_o = jnp.concatenate([oml[:, r, r * 64:(r + 1) * 64] for r in range(4)], axis=-1)
    lam_init = 0.8 - 0.6 * math.exp(-0.3 * l)
    xf = _merge_call(xs, sb_o, jnp.concatenate([df0, df1], axis=-1), fx_o, ml_o, lw, lam_init, N)
    xf = _moe_call(xf, lw, N)
    return xf, _rows([o[None] for o in (sbq, sbkv, dfq, dfk, dfv, fxq, fxkv, misc, mq, lat, mk, mv)], 1, N)


def kernel(x_prompt, x_sample, cache_sb_kv, cache_df_k, cache_df_v, cache_fx_kv, cache_fx_logf, cache_mla_latent, cache_mla_krope, page_table, meta_tokens, norm_mix, norm_ffn, w_in, w_gate, w_branch, w_out, df_q_norm, df_k_norm, df_lambda, df_out_norm, fx_q_norm, fx_k_norm, fx_forget_bias, mla_q_lora_norm, mla_kv_lora_norm, mla_w_uq, mla_w_ukv, mla_q_norm, mla_k_norm, moe_w_group, moe_b_group, moe_w_router, moe_b_router, moe_w_gate_up, moe_w_down):
    P = dict(norm_mix=norm_mix, norm_ffn=norm_ffn, w_in=w_in, w_gate=w_gate, w_branch=w_branch, w_out=w_out,
             df_q_norm=df_q_norm, df_k_norm=df_k_norm, df_lambda=df_lambda, df_out_norm=df_out_norm,
             fx_q_norm=fx_q_norm, fx_k_norm=fx_k_norm, fx_forget_bias=fx_forget_bias,
             mla_q_lora_norm=mla_q_lora_norm, mla_kv_lora_norm=mla_kv_lora_norm, mla_w_uq=mla_w_uq,
             mla_w_ukv=mla_w_ukv, mla_q_norm=mla_q_norm, mla_k_norm=mla_k_norm, moe_w_group=moe_w_group,
             moe_b_group=moe_b_group, moe_w_router=moe_w_router, moe_b_router=moe_b_router,
             moe_w_gate_up=moe_w_gate_up, moe_w_down=moe_w_down)
    depth = w_in.shape[0]
    B, S, D = x_prompt.shape
    L = S + N_META
    Lp = -(-L // TQ) * TQ
    lws = [_layer_weights(l, P) for l in range(depth)]

    meta = jnp.broadcast_to(meta_tokens[None].astype(x_prompt.dtype), (B, N_META, D))
    x = jnp.concatenate([meta, x_prompt, jnp.zeros((B, Lp - L, D), x_prompt.dtype)], axis=1)
    tab_p = _rope_table(jnp.arange(Lp, dtype=jnp.int32))
    rows_p = []
    for l in range(depth):
        x, rows = _prompt_layer(x, l, lws[l], tab_p, L)
        rows_p.append(rows)
    y_prompt = x[:, N_META:L]

    nb, dec_seq, _ = x_sample.shape
    assert dec_seq == 1
    n_pool, page = cache_sb_kv.shape[1], cache_sb_kv.shape[2]
    past_len = page_table.shape[1] * page
    tab_s = _rope_table(jnp.full((nb,), past_len, dtype=jnp.int32))
    assert page == LANES and n_pool % 2 == 0
    key_minor = (0, 1, 3, 4, 5, 2)
    caches = (cache_sb_kv.transpose(key_minor), cache_df_k.transpose(key_minor),
              cache_df_v.reshape(depth, n_pool, 2 * page, 128), cache_fx_kv.transpose(key_minor),
              cache_mla_latent, cache_mla_krope.transpose(0, 1, 3, 2),
              cache_fx_logf.transpose(0, 1, 3, 2).reshape(depth, n_pool // 2, SUBLANES, LANES))
    ind = _seg_ones(SUBLANES, 1, 256, 64)
    xs = x_sample.reshape(nb, D)
    rows_s = []
    for l in range(depth):
        xs, rows = _sample_layer(xs, l, lws[l], tab_s, page_table, caches, ind)
        rows_s.append(tuple(r.reshape((nb, 1) + r.shape[2:]) for r in rows))
    y_sample = xs.reshape(nb, 1, D)

    def stack(rl):
        return tuple(jnp.stack([r[i] for r in rl], axis=0) for i in range(len(rl[0])))

    sb_kv_p, df_k_p, df_v_p, fx_kv_p, fx_logf_p, mla_lat_p, mla_kr_p = stack(rows_p)
    sb_kv_s, df_k_s, df_v_s, fx_kv_s, fx_logf_s, mla_lat_s, mla_kr_s = stack(rows_s)
    return (y_prompt, y_sample, sb_kv_p, sb_kv_s, df_k_p, df_k_s, df_v_p, df_v_s, fx_kv_p, fx_kv_s,
            fx_logf_p, fx_logf_s, mla_lat_p, mla_lat_s, mla_kr_p, mla_kr_s)
```
